```python
import jax, jax.numpy as jnp
from jax import lax
import numpy as np

D_MODEL = 1024
BATCH = 4
SEQ = 4096
DEPTH = 4

CHUNK = 64
QBLK = 128
EPS = 1e-6
LN_EPS = 1e-5
MAX_POS_OFFSET = 32768

MLA_HEADS = 8
MLA_Q_RANK = 256
MLA_KV_RANK = 128
MLA_NOPE = 64
MLA_ROPE = 32
MLA_VDIM = 64
MLA_WIDTH = MLA_HEADS * MLA_VDIM
ROPE_THETA = 10000.0

GM_GROUPS = 4
GM_GROUP_CH = 128
GM_WIDTH = GM_GROUPS * GM_GROUP_CH
GM_BLOCK = 128

RW_HEADS = 8
RW_HEAD = 64
RW_WIDTH = RW_HEADS * RW_HEAD
RW_DECAY_LORA = 64
RW_A_LORA = 64
RW_V_LORA = 32
RW_LN_EPS = 64e-5

N_BRANCH = 3
BRANCH_WIDTH = 512

GM_IN = 2 * GM_WIDTH
RW_IN = 3 * RW_WIDTH + RW_DECAY_LORA + RW_A_LORA
Z_IN = N_BRANCH * BRANCH_WIDTH
GATE_IN = N_BRANCH * D_MODEL
IN_SPLITS = [MLA_Q_RANK, MLA_KV_RANK, MLA_ROPE, GM_IN, RW_IN, Z_IN, GATE_IN]
D_IN = sum(IN_SPLITS)

kernel_name = "hybrid_mla_gmlp_rwkv7_streaming_trunk"


def rmsnorm(x, g, eps=EPS):
    xf = x.astype(jnp.float32)
    y = xf * lax.rsqrt(jnp.mean(xf * xf, axis=-1, keepdims=True) + eps)
    return (y * g.astype(jnp.float32)).astype(x.dtype)


def layernorm(x, g, b, eps=LN_EPS):
    xf = x.astype(jnp.float32)
    mu = jnp.mean(xf, axis=-1, keepdims=True)
    var = jnp.mean(jnp.square(xf - mu), axis=-1, keepdims=True)
    y = (xf - mu) * lax.rsqrt(var + eps)
    return (y * g.astype(jnp.float32) + b.astype(jnp.float32)).astype(x.dtype)


def split_last(x, sizes):
    cuts = [int(s) for s in np.cumsum(sizes)[:-1]]
    return jnp.split(x, cuts, axis=-1)


def rope_angles(positions):
    inv_freq = ROPE_THETA ** (-jnp.arange(0, MLA_ROPE, 2, dtype=jnp.float32) / MLA_ROPE)
    ang = positions.astype(jnp.float32)[..., None] * inv_freq
    return jnp.cos(ang), jnp.sin(ang)


def apply_rope(x, cos, sin):
    xf = x.astype(jnp.float32)
    x1, x2 = jnp.split(xf, 2, axis=-1)
    return jnp.concatenate([x1 * cos - x2 * sin, x2 * cos + x1 * sin], axis=-1).astype(x.dtype)


def chunk_causal_attention(q, k, v):
    B, S, H, dk = q.shape
    dv = v.shape[-1]
    nb = S // QBLK
    scale = dk ** -0.5
    key_chunk = jnp.arange(S) // CHUNK
    qb = q.reshape(B, nb, QBLK, H, dk).transpose(1, 0, 2, 3, 4)

    def one_block(args):
        q_blk, blk = args
        q_chunk = (blk * QBLK + jnp.arange(QBLK)) // CHUNK
        allowed = key_chunk[None, :] <= q_chunk[:, None]
        s = jnp.einsum('bqhd,bkhd->bhqk', q_blk, k).astype(jnp.float32) * scale
        s = jnp.where(allowed[None, None], s, jnp.finfo(jnp.float32).min)
        p = jax.nn.softmax(s, axis=-1).astype(v.dtype)
        return jnp.einsum('bhqk,bkhd->bqhd', p, v)

    out = lax.map(one_block, (qb, jnp.arange(nb)))
    return out.transpose(1, 0, 2, 3, 4).reshape(B, S, H, dv)


def mla_mixer(q_lat, kv_lat, k_rope, q_norm_g, w_uq, kv_norm_g, w_ukv, cos, sin):
    B, S, _ = q_lat.shape
    q = (rmsnorm(q_lat, q_norm_g) @ w_uq).reshape(B, S, MLA_HEADS, MLA_NOPE + MLA_ROPE)
    q_nope, q_pe = q[..., :MLA_NOPE], q[..., MLA_NOPE:]
    q_pe = apply_rope(q_pe, cos[:, :, None, :], sin[:, :, None, :])
    kv = (rmsnorm(kv_lat, kv_norm_g) @ w_ukv).reshape(B, S, MLA_HEADS, MLA_NOPE + MLA_VDIM)
    k_nope, v = kv[..., :MLA_NOPE], kv[..., MLA_NOPE:]
    k_pe = apply_rope(k_rope, cos, sin)
    k_pe = jnp.broadcast_to(k_pe[:, :, None, :], (B, S, MLA_HEADS, MLA_ROPE))
    q_full = jnp.concatenate([q_nope, q_pe], axis=-1)
    k_full = jnp.concatenate([k_nope, k_pe], axis=-1)
    return chunk_causal_attention(q_full, k_full, v).reshape(B, S, MLA_WIDTH)


def gmlp_mixer(p, ln_g, ln_b, w_s, b_s):
    B, S, _ = p.shape
    u, v = jnp.split(jax.nn.gelu(p, approximate=False), 2, axis=-1)
    v = layernorm(v, ln_g, ln_b)
    pos_chunk = jnp.arange(GM_BLOCK) // CHUNK
    mask = pos_chunk[None, :] <= pos_chunk[:, None]
    w = jnp.where(mask[None], w_s, 0)
    vb = v.reshape(B, S // GM_BLOCK, GM_BLOCK, GM_GROUPS, GM_GROUP_CH)
    s = jnp.einsum('gts,bnsgc->bntgc', w, vb) + b_s.T[None, None, :, :, None]
    return u * s.reshape(B, S, GM_WIDTH)


def wkv7_scan(r, decay, k, v, a, b):
    B, S, H, N = r.shape
    xs = tuple(jnp.moveaxis(t.astype(jnp.float32), 1, 0) for t in (r, decay, k, v, a, b))

    def step(state, inp):
        r_t, w_t, k_t, v_t, a_t, b_t = inp
        sa = jnp.einsum('bhij,bhj->bhi', state, a_t)
        state = (state * w_t[:, :, None, :] + sa[..., None] * b_t[:, :, None, :]
                 + v_t[..., None] * k_t[:, :, None, :])
        return state, jnp.einsum('bhij,bhj->bhi', state, r_t)

    s0 = jnp.zeros((B, H, N, N), jnp.float32)
    _, ys = lax.scan(step, s0, xs)
    return jnp.moveaxis(ys, 0, 1).astype(r.dtype)


def rwkv7_mixer(p, mu, w0, w2, a0, a2, k_k, k_a, r_k, lnx_g, lnx_b, v_first, v_mix):
    B, S, _ = p.shape
    p_prev = jnp.pad(p, ((0, 0), (1, 0), (0, 0)))[:, :-1]
    xs = p + (p_prev - p) * mu
    r, k, v, w_lat, a_lat = split_last(xs, [RW_WIDTH, RW_WIDTH, RW_WIDTH, RW_DECAY_LORA, RW_A_LORA])
    w_log = -jax.nn.softplus(-(w0 + jnp.tanh(w_lat) @ w2).astype(jnp.float32)) - 0.5
    decay = jnp.exp(-jnp.exp(w_log))
    if v_mix is None:
        v_first = v
    else:
        v0, v1, v2 = v_mix
        v = v + (v_first - v) * jax.nn.sigmoid(v0 + (v @ v1) @ v2)
    a = jax.nn.sigmoid(a0 + a_lat @ a2)

    def heads(t):
        return t.reshape(B, S, RW_HEADS, RW_HEAD)

    kk = heads(k * k_k).astype(jnp.float32)
    kk = kk / jnp.maximum(jnp.sqrt(jnp.sum(kk * kk, axis=-1, keepdims=True)), 1e-12)
    k = k * (1 + (a - 1) * k_a)
    rh, kh, vh, ah = heads(r), heads(k), heads(v), heads(a)
    y = wkv7_scan(rh, heads(decay), kh, vh, -kk, kk * ah.astype(jnp.float32))
    y = layernorm(y, lnx_g.reshape(RW_HEADS, RW_HEAD), lnx_b.reshape(RW_HEADS, RW_HEAD), eps=RW_LN_EPS)
    bonus = jnp.sum(rh * kh * r_k, axis=-1, keepdims=True) * vh
    return (y + bonus).reshape(B, S, RW_WIDTH), v_first


def setup_inputs(seed: int = 0) -> dict:
    key = jax.random.key(seed)
    ks = jax.random.split(key, 32)
    L = DEPTH
    Lv = max(DEPTH - 1, 0)
    D = D_MODEL

    def nrm(i, shape, scale):
        return jax.random.normal(ks[i], shape, jnp.float32) * scale

    x = nrm(0, (BATCH, SEQ, D), 1.0)
    c = nrm(1, (BATCH, D), 1.0)
    offset = jax.random.randint(ks[2], (BATCH, 1), 0, MAX_POS_OFFSET, dtype=jnp.int32)
    positions = offset + jnp.arange(SEQ, dtype=jnp.int32)[None, :]
    return {
        "x": x,
        "c": c,
        "positions": positions,
        "pre_g": 1.0 + nrm(3, (L, D), 0.02),
        "post_g": 1.0 + nrm(4, (L, D), 0.02),
        "w_ada": nrm(5, (L, D, 3 * D), 0.5 * D ** -0.5),
        "b_ada": nrm(6, (L, 3 * D), 0.02),
        "w_in": nrm(7, (L, D, D_IN), D ** -0.5),
        "mla_q_norm": 1.0 + nrm(8, (L, MLA_Q_RANK), 0.02),
        "mla_w_uq": nrm(9, (L, MLA_Q_RANK, MLA_HEADS * (MLA_NOPE + MLA_ROPE)), MLA_Q_RANK ** -0.5),
        "mla_kv_norm": 1.0 + nrm(10, (L, MLA_KV_RANK), 0.02),
        "mla_w_ukv": nrm(11, (L, MLA_KV_RANK, MLA_HEADS * (MLA_NOPE + MLA_VDIM)), MLA_KV_RANK ** -0.5),
        "gm_ln_g": 1.0 + nrm(12, (L, GM_WIDTH), 0.02),
        "gm_ln_b": nrm(13, (L, GM_WIDTH), 0.02),
        "gm_w_s": nrm(14, (L, GM_GROUPS, GM_BLOCK, GM_BLOCK), GM_BLOCK ** -0.5),
        "gm_b_s": 1.0 + nrm(15, (L, GM_GROUPS, GM_BLOCK), 0.02),
        "rw_mu": jax.random.uniform(ks[16], (L, RW_IN), jnp.float32),
        "rw_w0": jax.random.uniform(ks[17], (L, RW_WIDTH), jnp.float32, minval=-3.0, maxval=0.5),
        "rw_w2": nrm(18, (L, RW_DECAY_LORA, RW_WIDTH), 0.1),
        "rw_a0": nrm(19, (L, RW_WIDTH), 0.1),
        "rw_a2": nrm(20, (L, RW_A_LORA, RW_WIDTH), RW_A_LORA ** -0.5),
        "rw_k_k": 0.85 + nrm(21, (L, RW_WIDTH), 0.02),
        "rw_k_a": 1.0 + nrm(22, (L, RW_WIDTH), 0.02),
        "rw_r_k": nrm(23, (L, RW_HEADS, RW_HEAD), 0.1),
        "rw_lnx_g": 1.0 + nrm(24, (L, RW_WIDTH), 0.02),
        "rw_lnx_b": nrm(25, (L, RW_WIDTH), 0.02),
        "rw_v0": 1.0 + nrm(26, (Lv, RW_WIDTH), 0.1),
        "rw_v1": nrm(27, (Lv, RW_WIDTH, RW_V_LORA), RW_WIDTH ** -0.5),
        "rw_v2": nrm(28, (Lv, RW_V_LORA, RW_WIDTH), 0.5 * RW_V_LORA ** -0.5),
        "w_br": nrm(29, (L, N_BRANCH, BRANCH_WIDTH, D), BRANCH_WIDTH ** -0.5),
        "w_out": nrm(30, (L, D, D), D ** -0.5),
    }


def reference(x, c, positions, pre_g, post_g, w_ada, b_ada, w_in,
              mla_q_norm, mla_w_uq, mla_kv_norm, mla_w_ukv,
              gm_ln_g, gm_ln_b, gm_w_s, gm_b_s,
              rw_mu, rw_w0, rw_w2, rw_a0, rw_a2, rw_k_k, rw_k_a, rw_r_k,
              rw_lnx_g, rw_lnx_b, rw_v0, rw_v1, rw_v2,
              w_br, w_out):
    B, S, D = x.shape
    cos, sin = rope_angles(positions)
    c_act = jax.nn.silu(c)
    v_first = None
    for l in range(DEPTH):
        shift, scale, gate = jnp.split(c_act @ w_ada[l] + b_ada[l], 3, axis=-1)
        h = rmsnorm(x, pre_g[l]) * (1 + scale[:, None]) + shift[:, None]
        q_lat, kv_lat, k_rope, gm_in, rw_in, z, g_logit = split_last(h @ w_in[l], IN_SPLITS)

        y_mla = mla_mixer(q_lat, kv_lat, k_rope, mla_q_norm[l], mla_w_uq[l],
                          mla_kv_norm[l], mla_w_ukv[l], cos, sin)
        y_gm = gmlp_mixer(gm_in, gm_ln_g[l], gm_ln_b[l], gm_w_s[l], gm_b_s[l])
        v_mix = None if l == 0 else (rw_v0[l - 1], rw_v1[l - 1], rw_v2[l - 1])
        y_rw, v_first = rwkv7_mixer(rw_in, rw_mu[l], rw_w0[l], rw_w2[l], rw_a0[l], rw_a2[l],
                                    rw_k_k[l], rw_k_a[l], rw_r_k[l], rw_lnx_g[l], rw_lnx_b[l],
                                    v_first, v_mix)

        branches = jnp.stack([y_mla, y_gm, y_rw], axis=2)
        branches = branches * jax.nn.silu(z).reshape(B, S, N_BRANCH, BRANCH_WIDTH)
        proj = jnp.einsum('bsnw,nwd->bsnd', branches, w_br[l])
        merged = jnp.sum(proj * jax.nn.sigmoid(g_logit).reshape(B, S, N_BRANCH, D), axis=2)
        y = merged @ w_out[l]
        x = x + gate[:, None] * rmsnorm(y, post_g[l])
    return x
```

```python
import functools
import math

import jax
import jax.numpy as jnp
import numpy as np
from jax import lax
from jax.experimental import pallas as pl
from jax.experimental.pallas import tpu as pltpu

F32 = jnp.float32
BF16 = jnp.bfloat16
HIGHEST = lax.Precision.HIGHEST

CHUNK = 64
EPS = 1e-6
LN_EPS = 1e-5
MLA_HEADS = 8
MLA_Q_RANK = 256
MLA_KV_RANK = 128
MLA_NOPE = 64
MLA_ROPE = 32
MLA_VDIM = 64
ROPE_THETA = 10000.0
GM_GROUPS = 4
GM_GROUP_CH = 128
GM_WIDTH = 512
GM_BLOCK = 128
RW_HEADS = 8
RW_HEAD = 64
RW_WIDTH = 512
RW_LORA = 64
RW_V_LORA = 32
RW_LN_EPS = 64e-5
BW = 512
LANE = 128
HEAD_PAD = 128

OFF_GM = 0
OFF_R, OFF_K, OFF_V = 1024, 1536, 2048
OFF_Z = 2560
OFF_G = 4096
OFF_QLAT = 7168
OFF_KVLAT = 7424
OFF_KR = 7552
OFF_KRS = 7680
OFF_LAT = 7808
NP = 8192

VMEM_LIMIT = 56 * 1024 * 1024


def _cparams(sem):
    return pltpu.CompilerParams(dimension_semantics=sem, vmem_limit_bytes=VMEM_LIMIT)


def _mm(a, b):
    return jnp.dot(a, b, preferred_element_type=F32, precision=HIGHEST)


def _mm_nt(a, b):
    return lax.dot_general(a, b, (((1,), (1,)), ((), ())), preferred_element_type=F32, precision=HIGHEST)


def _mm_tn(a, b):
    return lax.dot_general(a, b, (((0,), (0,)), ((), ())), preferred_element_type=F32, precision=HIGHEST)


def _bdot(a, b):
    return jnp.dot(a, b, preferred_element_type=F32)


def _sigmoid(x):
    return 1.0 / (1.0 + jnp.exp(-x))


def _ada_kernel(c_ref, w_ref, b_ref, o_ref):
    c = c_ref[...]
    ca = c * _sigmoid(c)
    o_ref[0] = _mm(ca, w_ref[0]) + b_ref[0]


def _ada(c_pad, w_ada, b_ada):
    L, D, D3 = w_ada.shape
    tn = 1024
    return pl.pallas_call(
        _ada_kernel,
        grid=(L, D3 // tn),
        in_specs=[
            pl.BlockSpec((8, D), lambda l, j: (0, 0)),
            pl.BlockSpec((1, D, tn), lambda l, j: (l, 0, j)),
            pl.BlockSpec((1, 1, tn), lambda l, j: (l, 0, j)),
        ],
        out_specs=pl.BlockSpec((1, 8, tn), lambda l, j: (l, 0, j)),
        out_shape=jax.ShapeDtypeStruct((L, 8, D3), F32),
        compiler_params=_cparams(("arbitrary", "arbitrary")),
        name="ada",
    )(c_pad, w_ada, b_ada.reshape(L, 1, D3))


def _rope_kernel(pos_ref, freq_ref, sign_ref, c_ref, s_ref):
    pos = pos_ref[...].astype(F32)
    ang = pos * freq_ref[...]
    lane = lax.broadcasted_iota(jnp.int32, ang.shape, 1)
    is_rope = (lane >= MLA_NOPE) & (lane < MLA_NOPE + MLA_ROPE)
    c_ref[...] = jnp.where(is_rope, jnp.cos(ang), jnp.where(lane < MLA_NOPE, 1.0, 0.0))
    s_ref[...] = jnp.sin(ang) * sign_ref[...]


def _rope_tables(positions):
    T = positions.size
    tm = 2048
    inv_freq = ROPE_THETA ** (-np.arange(0, MLA_ROPE, 2, dtype=np.float32) / MLA_ROPE)
    half = MLA_ROPE // 2
    freq = np.zeros((1, LANE), np.float32)
    freq[0, MLA_NOPE:MLA_NOPE + half] = inv_freq
    freq[0, MLA_NOPE + half:MLA_NOPE + MLA_ROPE] = inv_freq
    sign = np.zeros((1, LANE), np.float32)
    sign[0, MLA_NOPE:MLA_NOPE + half] = -1.0
    sign[0, MLA_NOPE + half:MLA_NOPE + MLA_ROPE] = 1.0
    return pl.pallas_call(
        _rope_kernel,
        grid=(T // tm,),
        in_specs=[
            pl.BlockSpec((tm, 1), lambda i: (i, 0)),
            pl.BlockSpec((1, LANE), lambda i: (0, 0)),
            pl.BlockSpec((1, LANE), lambda i: (0, 0)),
        ],
        out_specs=[pl.BlockSpec((tm, LANE), lambda i: (i, 0))] * 2,
        out_shape=[jax.ShapeDtypeStruct((T, LANE), F32)] * 2,
        compiler_params=_cparams(("arbitrary",)),
        name="rope_tables",
    )(positions.reshape(T, 1), jnp.asarray(freq), jnp.asarray(sign))


def _inproj_kernel(x_ref, sc_ref, sh_ref, g_ref, w_ref, o_ref, h_ref):
    @pl.when(pl.program_id(1) == 0)
    def _():
        x = x_ref[...]
        ms = jnp.mean(x * x, axis=-1, keepdims=True)
        y = x * lax.rsqrt(ms + EPS) * g_ref[...]
        h_ref[...] = (y * (1.0 + sc_ref[0]) + sh_ref[0]).astype(BF16)

    o_ref[...] = _bdot(h_ref[...], w_ref[...])


def _inproj(x2d, scale, shift, pre_g, w_in_p, seq):
    T, D = x2d.shape
    tm, tn = 1024, 1024
    per_b = seq // tm
    return pl.pallas_call(
        _inproj_kernel,
        grid=(T // tm, NP // tn),
        in_specs=[
            pl.BlockSpec((tm, D), lambda i, j: (i, 0)),
            pl.BlockSpec((1, 1, D), lambda i, j: (i // per_b, 0, 0)),
            pl.BlockSpec((1, 1, D), lambda i, j: (i // per_b, 0, 0)),
            pl.BlockSpec((1, D), lambda i, j: (0, 0)),
            pl.BlockSpec((D, tn), lambda i, j: (0, j)),
        ],
        out_specs=pl.BlockSpec((tm, tn), lambda i, j: (i, j)),
        out_shape=jax.ShapeDtypeStruct((T, NP), F32),
        scratch_shapes=[pltpu.VMEM((tm, D), BF16)],
        compiler_params=_cparams(("arbitrary", "arbitrary")),
        name="inproj",
    )(x2d, scale, shift, pre_g, w_in_p)


def _mla_prep_kernel(ql_ref, kvl_ref, kr_ref, krs_ref, ct_ref, st_ref, qg_ref, kvg_ref,
                     wq_ref, wqr_ref, wk_ref, wv_ref, q_ref, k_ref, v_ref, *, scale):
    ql = ql_ref[...]
    qn = (ql * lax.rsqrt(jnp.mean(ql * ql, axis=-1, keepdims=True) + EPS) * qg_ref[...]).astype(BF16)
    kvl = kvl_ref[...]
    kvn = (kvl * lax.rsqrt(jnp.mean(kvl * kvl, axis=-1, keepdims=True) + EPS) * kvg_ref[...]).astype(BF16)
    ct = ct_ref[...]
    st = st_ref[...]
    kpe = kr_ref[...] * ct + krs_ref[...] * st
    v_ref[...] = _bdot(kvn, wv_ref[...]).astype(BF16)
    for h in range(MLA_HEADS):
        sl = slice(h * HEAD_PAD, (h + 1) * HEAD_PAD)
        q = _bdot(qn, wq_ref[:, sl]) * ct + _bdot(qn, wqr_ref[:, sl]) * st
        q_ref[:, sl] = (q * scale).astype(BF16)
        k_ref[:, sl] = (_bdot(kvn, wk_ref[:, sl]) + kpe).astype(BF16)


def _mla_prep(p, ctab, stab, qg, kvg, wq, wqr, wk, wv):
    T = p.shape[0]
    tm = 512
    HP = MLA_HEADS * HEAD_PAD
    scale = float((MLA_NOPE + MLA_ROPE) ** -0.5)
    row = lambda blk: (lambda i: (i, blk))
    const = lambda i: (0, 0)
    return pl.pallas_call(
        functools.partial(_mla_prep_kernel, scale=scale),
        grid=(T // tm,),
        in_specs=[
            pl.BlockSpec((tm, MLA_Q_RANK), row(OFF_QLAT // MLA_Q_RANK)),
            pl.BlockSpec((tm, LANE), row(OFF_KVLAT // LANE)),
            pl.BlockSpec((tm, LANE), row(OFF_KR // LANE)),
            pl.BlockSpec((tm, LANE), row(OFF_KRS // LANE)),
            pl.BlockSpec((tm, LANE), row(0)),
            pl.BlockSpec((tm, LANE), row(0)),
            pl.BlockSpec((1, MLA_Q_RANK), const),
            pl.BlockSpec((1, MLA_KV_RANK), const),
            pl.BlockSpec((MLA_Q_RANK, HP), const),
            pl.BlockSpec((MLA_Q_RANK, HP), const),
            pl.BlockSpec((MLA_KV_RANK, HP), const),
            pl.BlockSpec((MLA_KV_RANK, MLA_HEADS * MLA_VDIM), const),
        ],
        out_specs=[
            pl.BlockSpec((tm, HP), row(0)),
            pl.BlockSpec((tm, HP), row(0)),
            pl.BlockSpec((tm, MLA_HEADS * MLA_VDIM), row(0)),
        ],
        out_shape=[
            jax.ShapeDtypeStruct((T, HP), BF16),
            jax.ShapeDtypeStruct((T, HP), BF16),
            jax.ShapeDtypeStruct((T, MLA_HEADS * MLA_VDIM), BF16),
        ],
        compiler_params=_cparams(("arbitrary",)),
        name="mla_prep",
    )(p, p, p, p, ctab, stab, qg, kvg, wq, wqr, wk, wv)


def _attn_kernel(q_ref, k_ref, v_ref, o_ref, *, tq, tk):
    qi = pl.program_id(2)
    n_full = (qi * tq) // tk
    n_diag = tq // tk
    q_pos = qi * tq + lax.broadcasted_iota(jnp.int32, (tq, tk), 0)
    k_off = lax.broadcasted_iota(jnp.int32, (tq, tk), 1)
    lane = lax.broadcasted_iota(jnp.int32, (tq, LANE), 1)
    outs = []
    for hh in range(2):
        hsl = slice(hh * HEAD_PAD, (hh + 1) * HEAD_PAD)
        q = q_ref[:, hsl]

        def step(ki, carry, masked):
            m, l, acc = carry
            start = pl.multiple_of(ki * tk, tk)
            k = k_ref[pl.ds(start, tk), hsl]
            s = lax.dot_general(q, k, (((1,), (1,)), ((), ())), preferred_element_type=F32)
            if masked:
                allowed = ((start + k_off) // CHUNK) <= (q_pos // CHUNK)
                s = jnp.where(allowed, s, -1e30)
            m_new = jnp.maximum(m, jnp.max(s, axis=-1, keepdims=True))
            pr = jnp.exp(s - m_new)
            alpha = jnp.exp(m - m_new)
            l_new = alpha * l + jnp.sum(pr, axis=-1, keepdims=True)
            v = v_ref[pl.ds(start, tk), :]
            acc_new = alpha * acc + _bdot(pr.astype(BF16), v)
            return m_new, l_new, acc_new

        carry = (jnp.full((tq, 1), -1e30, F32), jnp.zeros((tq, 1), F32), jnp.zeros((tq, LANE), F32))
        carry = lax.fori_loop(0, n_full, functools.partial(step, masked=False), carry)
        for d in range(n_diag):
            carry = step(n_full + d, carry, True)
        m, l, acc = carry
        outs.append(acc / l)
    o_ref[...] = jnp.where(lane < MLA_VDIM, outs[0], outs[1])


def _attention(q, k, v, batch, seq):
    T = q.shape[0]
    tq, tk = 512, 512
    nq = seq // tq
    npair = MLA_HEADS // 2
    return pl.pallas_call(
        functools.partial(_attn_kernel, tq=tq, tk=tk),
        grid=(batch, npair, nq),
        in_specs=[
            pl.BlockSpec((tq, 2 * HEAD_PAD), lambda b, h, i: (b * nq + i, h)),
            pl.BlockSpec((seq, 2 * HEAD_PAD), lambda b, h, i: (b, h)),
            pl.BlockSpec((seq, 2 * MLA_VDIM), lambda b, h, i: (b, h)),
        ],
        out_specs=pl.BlockSpec((tq, 2 * MLA_VDIM), lambda b, h, i: (b * nq + i, h)),
        out_shape=jax.ShapeDtypeStruct((T, MLA_HEADS * MLA_VDIM), F32),
        compiler_params=_cparams(("arbitrary", "arbitrary", "arbitrary")),
        name="attention",
    )(q, k, v)


def _gmlp_kernel(p_ref, g_ref, b_ref, w_ref, bs_ref, o_ref, *, tm):
    x = p_ref[...]
    ge = 0.5 * x * (1.0 + lax.erf(x * (1.0 / math.sqrt(2.0))))
    u = ge[:, :GM_WIDTH]
    v = ge[:, GM_WIDTH:]
    mu = jnp.mean(v, axis=-1, keepdims=True)
    vc = v - mu
    var = jnp.mean(vc * vc, axis=-1, keepdims=True)
    vn = (vc * lax.rsqrt(var + LN_EPS) * g_ref[...] + b_ref[...]).astype(BF16)
    row = lax.broadcasted_iota(jnp.int32, (GM_BLOCK, GM_BLOCK), 0)
    col = lax.broadcasted_iota(jnp.int32, (GM_BLOCK, GM_BLOCK), 1)
    mask = (col // CHUNK) <= (row // CHUNK)
    for g in range(GM_GROUPS):
        w = jnp.where(mask, w_ref[g], 0.0).astype(BF16)
        csl = slice(g * GM_GROUP_CH, (g + 1) * GM_GROUP_CH)
        for blk in range(tm // GM_BLOCK):
            rsl = slice(blk * GM_BLOCK, (blk + 1) * GM_BLOCK)
            s = _bdot(w, vn[rsl, csl]) + bs_ref[g]
            o_ref[rsl, csl] = u[rsl, csl] * s


def _gmlp(p, ln_g, ln_b, w_s, b_s):
    T = p.shape[0]
    tm = 512
    return pl.pallas_call(
        functools.partial(_gmlp_kernel, tm=tm),
        grid=(T // tm,),
        in_specs=[
            pl.BlockSpec((tm, 2 * GM_WIDTH), lambda i: (i, OFF_GM // (2 * GM_WIDTH))),
            pl.BlockSpec((1, GM_WIDTH), lambda i: (0, 0)),
            pl.BlockSpec((1, GM_WIDTH), lambda i: (0, 0)),
            pl.BlockSpec((GM_GROUPS, GM_BLOCK, GM_BLOCK), lambda i: (0, 0, 0)),
            pl.BlockSpec((GM_GROUPS, GM_BLOCK, 1), lambda i: (0, 0, 0)),
        ],
        out_specs=pl.BlockSpec((tm, GM_WIDTH), lambda i: (i, 0)),
        out_shape=jax.ShapeDtypeStruct((T, GM_WIDTH), F32),
        compiler_params=_cparams(("arbitrary",)),
        name="gmlp",
    )(p, ln_g, ln_b, w_s, b_s.reshape(GM_GROUPS, GM_BLOCK, 1))


def _split3_dot(x, m_bf16):
    x1 = x.astype(BF16)
    r1 = x - x1.astype(F32)
    x2 = r1.astype(BF16)
    x3 = (r1 - x2.astype(F32)).astype(BF16)
    return _bdot(x1, m_bf16) + _bdot(x2, m_bf16) + _bdot(x3, m_bf16)


def _rw_prep_kernel(*refs, tm, has_vmix):
    if has_vmix:
        (r_ref, k_ref, v_ref, lat_ref, vf_ref, mur_ref, muk_ref, muv_ref, mul_ref, w0_ref, w2_ref, a0_ref, a2_ref,
         kk_ref, ka_ref, v0_ref, v1_ref, v2_ref, ones_ref,
         ro_ref, lwo_ref, ko_ref, vo_ref, ao_ref, bo_ref, cr_ref, ck_ref, cv_ref, cl_ref) = refs
    else:
        (r_ref, k_ref, v_ref, lat_ref, mur_ref, muk_ref, muv_ref, mul_ref, w0_ref, w2_ref, a0_ref, a2_ref,
         kk_ref, ka_ref, ones_ref,
         ro_ref, lwo_ref, ko_ref, vo_ref, ao_ref, bo_ref, cr_ref, ck_ref, cv_ref, cl_ref) = refs

    @pl.when(pl.program_id(1) == 0)
    def _():
        for c in (cr_ref, ck_ref, cv_ref, cl_ref):
            c[...] = jnp.zeros_like(c)

    def shifted(x_ref, carry_ref, mu_ref):
        x = x_ref[...]
        rolled = pltpu.roll(x, 1, 0)
        rid = lax.broadcasted_iota(jnp.int32, x.shape, 0)
        prev = jnp.where(rid == 0, carry_ref[0:1, :], rolled)
        carry_ref[0:1, :] = x[tm - 1:tm, :]
        return x + (prev - x) * mu_ref[...]

    r = shifted(r_ref, cr_ref, mur_ref)
    k = shifted(k_ref, ck_ref, muk_ref)
    v = shifted(v_ref, cv_ref, muv_ref)
    lat = shifted(lat_ref, cl_ref, mul_ref)

    ww = w0_ref[...] + _mm(jnp.tanh(lat), w2_ref[...])
    nw = -ww
    softplus = jnp.maximum(nw, 0.0) + jnp.log(1.0 + jnp.exp(-jnp.abs(nw)))
    w_log = -softplus - 0.5
    lwo_ref[...] = -jnp.exp(w_log)
    a = _sigmoid(a0_ref[...] + _mm(lat, a2_ref[...]))
    if has_vmix:
        gate = _sigmoid(v0_ref[...] + _mm(_mm(v, v1_ref[...]), v2_ref[...]))
        v = v + (vf_ref[...] - v) * gate
    kk = k * kk_ref[...]
    n2 = _split3_dot(kk * kk, ones_ref[...])
    kk = kk / jnp.maximum(jnp.sqrt(n2), 1e-12)
    ro_ref[...] = r
    ko_ref[...] = k * (1.0 + (a - 1.0) * ka_ref[...])
    vo_ref[...] = v
    ao_ref[...] = -kk
    bo_ref[...] = kk * a


def _rw_prep(p, v_first, prm, batch, seq):
    T = p.shape[0]
    tm = 512
    nt = seq // tm
    has_vmix = v_first is not None
    W = RW_WIDTH
    rowp = lambda blk: (lambda b, j: (b * nt + j, blk))
    row0 = lambda b, j: (b * nt + j, 0)
    const = lambda b, j: (0, 0)
    vec = pl.BlockSpec((1, W), const)
    in_specs = [
        pl.BlockSpec((tm, W), rowp(OFF_R // W)),
        pl.BlockSpec((tm, W), rowp(OFF_K // W)),
        pl.BlockSpec((tm, W), rowp(OFF_V // W)),
        pl.BlockSpec((tm, LANE), rowp(OFF_LAT // LANE)),
    ]
    args = [p, p, p, p]
    if has_vmix:
        in_specs.append(pl.BlockSpec((tm, W), row0))
        args.append(v_first)
    in_specs += [vec, vec, vec, pl.BlockSpec((1, LANE), const), vec, pl.BlockSpec((LANE, W), const),
                 vec, pl.BlockSpec((LANE, W), const), vec, vec]
    args += [prm["mu_r"], prm["mu_k"], prm["mu_v"], prm["mu_l"], prm["w0"], prm["w2"], prm["a0"], prm["a2"],
             prm["k_k"], prm["k_a"]]
    if has_vmix:
        in_specs += [vec, pl.BlockSpec((W, LANE), const), pl.BlockSpec((LANE, W), const)]
        args += [prm["v0"], prm["v1"], prm["v2"]]
    in_specs.append(pl.BlockSpec((W, W), const))
    args.append(prm["head_ones"])
    out = pl.BlockSpec((tm, W), row0)
    return pl.pallas_call(
        functools.partial(_rw_prep_kernel, tm=tm, has_vmix=has_vmix),
        grid=(batch, nt),
        in_specs=in_specs,
        out_specs=[out] * 6,
        out_shape=[jax.ShapeDtypeStruct((T, W), F32)] * 6,
        scratch_shapes=[pltpu.VMEM((8, W), F32)] * 3 + [pltpu.VMEM((8, LANE), F32)],
        compiler_params=_cparams(("arbitrary", "arbitrary")),
        name="rw_prep",
    )(*args)


def _unit_lower_inverse(low, row, col, n):
    eye = jnp.where(row == col, 1.0, 0.0)
    base = 16
    same = (row // base) == (col // base)
    ld = jnp.where(same, low, 0.0)
    inv = eye + ld
    pw = ld
    span = 2
    while span < base:
        pw = _mm(pw, pw)
        inv = inv + _mm(inv, pw)
        span *= 2
    size = base
    while size < n:
        pair = (row // (2 * size)) == (col // (2 * size))
        off = jnp.where(pair, jnp.where((row // size) == (col // size), 0.0, low), 0.0)
        inv = inv + _mm(_mm(inv, off), inv)
        size *= 2
    return inv


def _wkv_kernel(r_ref, lw_ref, k_ref, v_ref, a_ref, b_ref, rk_ref, g_ref, beta_ref, o_ref, s_ref, *, C):
    @pl.when(pl.program_id(1) == 0)
    def _():
        s_ref[...] = jnp.zeros_like(s_ref)

    r = r_ref[...]
    lw = lw_ref[...]
    k = k_ref[...]
    v = v_ref[...]
    a_s = a_ref[...]
    b_s = b_ref[...]
    row = lax.broadcasted_iota(jnp.int32, (C, C), 0)
    col = lax.broadcasted_iota(jnp.int32, (C, C), 1)
    incl = row >= col
    strict = row > col
    cum = _mm(jnp.where(incl, 1.0, 0.0), lw)
    cum_last = cum[C - 1:C, :]
    e_in = jnp.exp(cum)
    e_inv = jnp.exp(-cum)
    e_end = jnp.exp(cum_last - cum)
    a_t = a_s * jnp.exp(cum - lw)
    r_t = r * e_in
    b_t = b_s * e_inv
    k_t = k * e_inv
    b_e = b_s * e_end
    k_e = k * e_end
    p_end = jnp.exp(cum_last)
    rk = r * k * rk_ref[...]
    for h in range(RW_HEADS):
        sl = slice(h * RW_HEAD, (h + 1) * RW_HEAD)
        s0 = s_ref[h]
        ah = _mm_nt(a_t[:, sl], s0)
        rh = _mm_nt(r_t[:, sl], s0)
        vh = v[:, sl]
        a_ab = jnp.where(strict, _mm_nt(a_t[:, sl], b_t[:, sl]), 0.0)
        a_ak = jnp.where(strict, _mm_nt(a_t[:, sl], k_t[:, sl]), 0.0)
        a_rb = jnp.where(incl, _mm_nt(r_t[:, sl], b_t[:, sl]), 0.0)
        a_rk = jnp.where(incl, _mm_nt(r_t[:, sl], k_t[:, sl]), 0.0)
        tinv = _unit_lower_inverse(a_ab, row, col, C)
        u = _mm(tinv, ah + _mm(a_ak, vh))
        y = rh + _mm(a_rb, u) + _mm(a_rk, vh)
        s_ref[h] = s0 * p_end[:, sl] + _mm_tn(u, b_e[:, sl]) + _mm_tn(vh, k_e[:, sl])
        mu = jnp.mean(y, axis=-1, keepdims=True)
        yc = y - mu
        var = jnp.mean(yc * yc, axis=-1, keepdims=True)
        yn = yc * lax.rsqrt(var + RW_LN_EPS) * g_ref[:, sl] + beta_ref[:, sl]
        bonus = jnp.sum(rk[:, sl], axis=-1, keepdims=True) * vh
        o_ref[:, sl] = yn + bonus


def _wkv(r, lw, k, v, a_s, b_s, r_k, lnx_g, lnx_b, batch, seq):
    T = r.shape[0]
    C = 64
    nc = seq // C
    W = RW_WIDTH
    row = pl.BlockSpec((C, W), lambda b, c: (b * nc + c, 0))
    vec = pl.BlockSpec((1, W), lambda b, c: (0, 0))
    return pl.pallas_call(
        functools.partial(_wkv_kernel, C=C),
        grid=(batch, nc),
        in_specs=[row] * 6 + [vec] * 3,
        out_specs=row,
        out_shape=jax.ShapeDtypeStruct((T, W), F32),
        scratch_shapes=[pltpu.VMEM((RW_HEADS, RW_HEAD, RW_HEAD), F32)],
        compiler_params=_cparams(("arbitrary", "arbitrary")),
        name="wkv7",
    )(r, lw, k, v, a_s, b_s, r_k, lnx_g, lnx_b)


def _merge_kernel(ya_ref, yg_ref, yr_ref, za_ref, zg_ref, zr_ref, ga_ref, gg_ref, gr_ref,
                  x_ref, gate_ref, pg_ref, wbr_ref, wout_ref, o_ref):
    acc = None
    for n, (y_ref, z_ref, g_ref) in enumerate(((ya_ref, za_ref, ga_ref), (yg_ref, zg_ref, gg_ref),
                                                (yr_ref, zr_ref, gr_ref))):
        z = z_ref[...]
        br = (y_ref[...] * (z * _sigmoid(z))).astype(BF16)
        pr = _bdot(br, wbr_ref[n]) * _sigmoid(g_ref[...])
        acc = pr if acc is None else acc + pr
    y = _bdot(acc.astype(BF16), wout_ref[...])
    yn = y * lax.rsqrt(jnp.mean(y * y, axis=-1, keepdims=True) + EPS) * pg_ref[...]
    o_ref[...] = x_ref[...] + gate_ref[0] * yn


def _merge(y_mla, y_gm, y_rw, p, x2d, gate, post_g, w_br, w_out, seq):
    T, D = x2d.shape
    tm = 256
    per_b = seq // tm
    rowp = lambda blk: (lambda i: (i, blk))
    yspec = pl.BlockSpec((tm, BW), rowp(0))
    return pl.pallas_call(
        _merge_kernel,
        grid=(T // tm,),
        in_specs=[yspec, yspec, yspec]
        + [pl.BlockSpec((tm, BW), rowp(OFF_Z // BW + n)) for n in range(3)]
        + [pl.BlockSpec((tm, D), rowp(OFF_G // D + n)) for n in range(3)]
        + [
            pl.BlockSpec((tm, D), rowp(0)),
            pl.BlockSpec((1, 1, D), lambda i: (i // per_b, 0, 0)),
            pl.BlockSpec((1, D), lambda i: (0, 0)),
            pl.BlockSpec((3, BW, D), lambda i: (0, 0, 0)),
            pl.BlockSpec((D, D), lambda i: (0, 0)),
        ],
        out_specs=pl.BlockSpec((tm, D), rowp(0)),
        out_shape=jax.ShapeDtypeStruct((T, D), F32),
        compiler_params=_cparams(("arbitrary",)),
        name="merge",
    )(y_mla, y_gm, y_rw, p, p, p, p, p, p, x2d, gate, post_g, w_br, w_out)


def _relayout_w_in(w_in):
    L, D, _ = w_in.shape
    o_q, o_kv, o_kr, o_gm = 0, MLA_Q_RANK, MLA_Q_RANK + MLA_KV_RANK, MLA_Q_RANK + MLA_KV_RANK + MLA_ROPE
    o_rw = o_gm + 2 * GM_WIDTH
    o_lat = o_rw + 3 * RW_WIDTH
    o_z = o_lat + 2 * RW_LORA
    o_g = o_z + 3 * BW
    half = MLA_ROPE // 2
    kr = w_in[:, :, o_kr:o_kr + MLA_ROPE]
    krs = jnp.concatenate([kr[:, :, half:], kr[:, :, :half]], axis=-1)
    z64 = jnp.zeros((L, D, MLA_NOPE), w_in.dtype)
    z32 = jnp.zeros((L, D, LANE - MLA_NOPE - MLA_ROPE), w_in.dtype)
    pad = jnp.zeros((L, D, NP - OFF_LAT - LANE), w_in.dtype)
    cols = [
        w_in[:, :, o_gm:o_lat],
        w_in[:, :, o_z:o_g],
        w_in[:, :, o_g:],
        w_in[:, :, o_q:o_kr],
        z64, kr, z32,
        z64, krs, z32,
        w_in[:, :, o_lat:o_z],
        pad,
    ]
    return jnp.concatenate(cols, axis=-1).astype(BF16)


def _relayout_mla(w_uq, w_ukv):
    L = w_uq.shape[0]
    dq = MLA_NOPE + MLA_ROPE
    half = MLA_ROPE // 2
    wq = w_uq.reshape(L, MLA_Q_RANK, MLA_HEADS, dq)
    padq = ((0, 0), (0, 0), (0, 0), (0, HEAD_PAD - dq))
    wq_p = jnp.pad(wq, padq).reshape(L, MLA_Q_RANK, MLA_HEADS * HEAD_PAD)
    rope = wq[..., MLA_NOPE:]
    rot = jnp.concatenate([jnp.zeros_like(wq[..., :MLA_NOPE]), rope[..., half:], rope[..., :half]], axis=-1)
    wqr_p = jnp.pad(rot, padq).reshape(L, MLA_Q_RANK, MLA_HEADS * HEAD_PAD)
    wkv = w_ukv.reshape(L, MLA_KV_RANK, MLA_HEADS, MLA_NOPE + MLA_VDIM)
    wk_p = jnp.pad(wkv[..., :MLA_NOPE], ((0, 0), (0, 0), (0, 0), (0, HEAD_PAD - MLA_NOPE)))
    wk_p = wk_p.reshape(L, MLA_KV_RANK, MLA_HEADS * HEAD_PAD)
    wv = wkv[..., MLA_NOPE:].reshape(L, MLA_KV_RANK, MLA_HEADS * MLA_VDIM)
    return wq_p.astype(BF16), wqr_p.astype(BF16), wk_p.astype(BF16), wv.astype(BF16)


def kernel(x, c, positions, pre_g, post_g, w_ada, b_ada, w_in, mla_q_norm, mla_w_uq, mla_kv_norm, mla_w_ukv,
           gm_ln_g, gm_ln_b, gm_w_s, gm_b_s, rw_mu, rw_w0, rw_w2, rw_a0, rw_a2, rw_k_k, rw_k_a, rw_r_k,
           rw_lnx_g, rw_lnx_b, rw_v0, rw_v1, rw_v2, w_br, w_out):
    B, S, D = x.shape
    L = w_in.shape[0]
    T = B * S
    x2d = x.reshape(T, D)

    c_pad = jnp.pad(c, ((0, 8 - B), (0, 0)))
    mods = _ada(c_pad, w_ada, b_ada)[:, :B]
    ctab, stab = _rope_tables(positions)

    w_in_p = _relayout_w_in(w_in)
    wq_p, wqr_p, wk_p, wv_p = _relayout_mla(mla_w_uq, mla_w_ukv)
    w_br_b = w_br.astype(BF16)
    w_out_b = w_out.astype(BF16)
    zl = jnp.zeros((RW_LORA, RW_WIDTH), F32)
    hid = np.arange(RW_WIDTH) // RW_HEAD
    head_ones = jnp.asarray((hid[:, None] == hid[None, :]).astype(np.float32)).astype(BF16)

    v_first = None
    for l in range(L):
        shift = mods[l, :, :D].reshape(B, 1, D)
        scale = mods[l, :, D:2 * D].reshape(B, 1, D)
        gate = mods[l, :, 2 * D:].reshape(B, 1, D)
        p = _inproj(x2d, scale, shift, pre_g[l].reshape(1, D), w_in_p[l], S)

        q, k, v = _mla_prep(p, ctab, stab, mla_q_norm[l].reshape(1, -1), mla_kv_norm[l].reshape(1, -1),
                            wq_p[l], wqr_p[l], wk_p[l], wv_p[l])
        y_mla = _attention(q, k, v, B, S)

        y_gm = _gmlp(p, gm_ln_g[l].reshape(1, -1), gm_ln_b[l].reshape(1, -1), gm_w_s[l], gm_b_s[l])

        mu = rw_mu[l]
        prm = {
            "mu_r": mu[:RW_WIDTH].reshape(1, -1),
            "mu_k": mu[RW_WIDTH:2 * RW_WIDTH].reshape(1, -1),
            "mu_v": mu[2 * RW_WIDTH:3 * RW_WIDTH].reshape(1, -1),
            "mu_l": mu[3 * RW_WIDTH:].reshape(1, -1),
            "w0": rw_w0[l].reshape(1, -1),
            "w2": jnp.concatenate([rw_w2[l], zl], axis=0),
            "a0": rw_a0[l].reshape(1, -1),
            "a2": jnp.concatenate([zl, rw_a2[l]], axis=0),
            "k_k": rw_k_k[l].reshape(1, -1),
            "k_a": rw_k_a[l].reshape(1, -1),
            "head_ones": head_ones,
        }
        if l > 0:
            prm["v0"] = rw_v0[l - 1].reshape(1, -1)
            prm["v1"] = jnp.pad(rw_v1[l - 1], ((0, 0), (0, LANE - RW_V_LORA)))
            prm["v2"] = jnp.pad(rw_v2[l - 1], ((0, LANE - RW_V_LORA), (0, 0)))
        r_s, lw_s, k_s, v_s, a_s, b_s = _rw_prep(p, v_first, prm, B, S)
        if l == 0:
            v_first = v_s
        y_rw = _wkv(r_s, lw_s, k_s, v_s, a_s, b_s, rw_r_k[l].reshape(1, -1), rw_lnx_g[l].reshape(1, -1),
                    rw_lnx_b[l].reshape(1, -1), B, S)

        x2d = _merge(y_mla, y_gm, y_rw, p, x2d, gate, post_g[l].reshape(1, D), w_br_b[l], w_out_b[l], S)
    return x2d.reshape(B, S, D)
```

```python
import functools
import math

import jax
import jax.numpy as jnp
import numpy as np
from jax import lax
from jax.experimental import pallas as pl
from jax.experimental.pallas import tpu as pltpu

F32 = jnp.float32
BF16 = jnp.bfloat16
HIGHEST = lax.Precision.HIGHEST

CHUNK = 64
EPS = 1e-6
LN_EPS = 1e-5
MLA_HEADS = 8
MLA_Q_RANK = 256
MLA_KV_RANK = 128
MLA_NOPE = 64
MLA_ROPE = 32
MLA_VDIM = 64
ROPE_THETA = 10000.0
GM_GROUPS = 4
GM_GROUP_CH = 128
GM_WIDTH = 512
GM_BLOCK = 128
RW_HEADS = 8
RW_HEAD = 64
RW_WIDTH = 512
RW_LORA = 64
RW_V_LORA = 32
RW_LN_EPS = 64e-5
BW = 512
LANE = 128
HEAD_PAD = 128

OFF_GM = 0
OFF_R, OFF_K, OFF_V = 1024, 1536, 2048
OFF_Z = 2560
OFF_G = 4096
OFF_QLAT = 7168
OFF_KVLAT = 7424
OFF_KR = 7552
OFF_KRS = 7680
OFF_LAT = 7808
NP = 8192

VMEM_LIMIT = 56 * 1024 * 1024


def _cparams(sem):
    return pltpu.CompilerParams(dimension_semantics=sem, vmem_limit_bytes=VMEM_LIMIT)


def _mm(a, b):
    return jnp.dot(a, b, preferred_element_type=F32, precision=HIGHEST)


_WKV_PASSES = 1


def _pdot(a, b, dims=(((1,), (0,)), ((), ()))):
    a1 = a.astype(BF16)
    b1 = b.astype(BF16)
    out = lax.dot_general(a1, b1, dims, preferred_element_type=F32)
    if _WKV_PASSES == 3:
        a2 = (a - a1.astype(F32)).astype(BF16)
        b2 = (b - b1.astype(F32)).astype(BF16)
        out = out + lax.dot_general(a1, b2, dims, preferred_element_type=F32)
        out = out + lax.dot_general(a2, b1, dims, preferred_element_type=F32)
    return out


def _bdot(a, b):
    return jnp.dot(a, b, preferred_element_type=F32)


def _sigmoid(x):
    return 1.0 / (1.0 + jnp.exp(-x))


def _ada_kernel(c_ref, w_ref, b_ref, o_ref):
    c = c_ref[...]
    ca = c * _sigmoid(c)
    o_ref[0] = _mm(ca, w_ref[0]) + b_ref[0]


def _ada(c_pad, w_ada, b_ada):
    L, D, D3 = w_ada.shape
    tn = 1024
    return pl.pallas_call(
        _ada_kernel,
        grid=(L, D3 // tn),
        in_specs=[
            pl.BlockSpec((8, D), lambda l, j: (0, 0)),
            pl.BlockSpec((1, D, tn), lambda l, j: (l, 0, j)),
            pl.BlockSpec((1, 1, tn), lambda l, j: (l, 0, j)),
        ],
        out_specs=pl.BlockSpec((1, 8, tn), lambda l, j: (l, 0, j)),
        out_shape=jax.ShapeDtypeStruct((L, 8, D3), F32),
        compiler_params=_cparams(("arbitrary", "arbitrary")),
        name="ada",
    )(c_pad, w_ada, b_ada.reshape(L, 1, D3))


def _rope_kernel(pos_ref, freq_ref, sign_ref, c_ref, s_ref):
    pos = pos_ref[...].astype(F32)
    ang = pos * freq_ref[...]
    lane = lax.broadcasted_iota(jnp.int32, ang.shape, 1)
    is_rope = (lane >= MLA_NOPE) & (lane < MLA_NOPE + MLA_ROPE)
    c_ref[...] = jnp.where(is_rope, jnp.cos(ang), jnp.where(lane < MLA_NOPE, 1.0, 0.0))
    s_ref[...] = jnp.sin(ang) * sign_ref[...]


def _rope_tables(positions):
    T = positions.size
    tm = 2048
    inv_freq = ROPE_THETA ** (-np.arange(0, MLA_ROPE, 2, dtype=np.float32) / MLA_ROPE)
    half = MLA_ROPE // 2
    freq = np.zeros((1, LANE), np.float32)
    freq[0, MLA_NOPE:MLA_NOPE + half] = inv_freq
    freq[0, MLA_NOPE + half:MLA_NOPE + MLA_ROPE] = inv_freq
    sign = np.zeros((1, LANE), np.float32)
    sign[0, MLA_NOPE:MLA_NOPE + half] = -1.0
    sign[0, MLA_NOPE + half:MLA_NOPE + MLA_ROPE] = 1.0
    return pl.pallas_call(
        _rope_kernel,
        grid=(T // tm,),
        in_specs=[
            pl.BlockSpec((tm, 1), lambda i: (i, 0)),
            pl.BlockSpec((1, LANE), lambda i: (0, 0)),
            pl.BlockSpec((1, LANE), lambda i: (0, 0)),
        ],
        out_specs=[pl.BlockSpec((tm, LANE), lambda i: (i, 0))] * 2,
        out_shape=[jax.ShapeDtypeStruct((T, LANE), F32)] * 2,
        compiler_params=_cparams(("arbitrary",)),
        name="rope_tables",
    )(positions.reshape(T, 1), jnp.asarray(freq), jnp.asarray(sign))


def _inproj_kernel(x_ref, sc_ref, sh_ref, g_ref, w_ref, o_ref, h_ref):
    @pl.when(pl.program_id(1) == 0)
    def _():
        x = x_ref[...]
        ms = jnp.mean(x * x, axis=-1, keepdims=True)
        y = x * lax.rsqrt(ms + EPS) * g_ref[...]
        h_ref[...] = (y * (1.0 + sc_ref[0]) + sh_ref[0]).astype(BF16)

    o_ref[...] = _bdot(h_ref[...], w_ref[...])


def _inproj(x2d, scale, shift, pre_g, w_in_p, seq):
    T, D = x2d.shape
    tm, tn = 1024, 1024
    per_b = seq // tm
    return pl.pallas_call(
        _inproj_kernel,
        grid=(T // tm, NP // tn),
        in_specs=[
            pl.BlockSpec((tm, D), lambda i, j: (i, 0)),
            pl.BlockSpec((1, 1, D), lambda i, j: (i // per_b, 0, 0)),
            pl.BlockSpec((1, 1, D), lambda i, j: (i // per_b, 0, 0)),
            pl.BlockSpec((1, D), lambda i, j: (0, 0)),
            pl.BlockSpec((D, tn), lambda i, j: (0, j)),
        ],
        out_specs=pl.BlockSpec((tm, tn), lambda i, j: (i, j)),
        out_shape=jax.ShapeDtypeStruct((T, NP), F32),
        scratch_shapes=[pltpu.VMEM((tm, D), BF16)],
        compiler_params=_cparams(("arbitrary", "arbitrary")),
        name="inproj",
    )(x2d, scale, shift, pre_g, w_in_p)


def _mla_prep_kernel(ql_ref, kvl_ref, kr_ref, krs_ref, ct_ref, st_ref, qg_ref, kvg_ref,
                     wq_ref, wqr_ref, wk_ref, wv_ref, q_ref, k_ref, v_ref, *, scale):
    ql = ql_ref[...]
    qn = (ql * lax.rsqrt(jnp.mean(ql * ql, axis=-1, keepdims=True) + EPS) * qg_ref[...]).astype(BF16)
    kvl = kvl_ref[...]
    kvn = (kvl * lax.rsqrt(jnp.mean(kvl * kvl, axis=-1, keepdims=True) + EPS) * kvg_ref[...]).astype(BF16)
    ct = ct_ref[...]
    st = st_ref[...]
    kpe = kr_ref[...] * ct + krs_ref[...] * st
    v_ref[...] = _bdot(kvn, wv_ref[...]).astype(BF16)
    for h in range(MLA_HEADS):
        sl = slice(h * HEAD_PAD, (h + 1) * HEAD_PAD)
        q = _bdot(qn, wq_ref[:, sl]) * ct + _bdot(qn, wqr_ref[:, sl]) * st
        q_ref[:, sl] = (q * scale).astype(BF16)
        k_ref[:, sl] = (_bdot(kvn, wk_ref[:, sl]) + kpe).astype(BF16)


def _mla_prep(p, ctab, stab, qg, kvg, wq, wqr, wk, wv):
    T = p.shape[0]
    tm = 512
    HP = MLA_HEADS * HEAD_PAD
    scale = float((MLA_NOPE + MLA_ROPE) ** -0.5)
    row = lambda blk: (lambda i: (i, blk))
    const = lambda i: (0, 0)
    return pl.pallas_call(
        functools.partial(_mla_prep_kernel, scale=scale),
        grid=(T // tm,),
        in_specs=[
            pl.BlockSpec((tm, MLA_Q_RANK), row(OFF_QLAT // MLA_Q_RANK)),
            pl.BlockSpec((tm, LANE), row(OFF_KVLAT // LANE)),
            pl.BlockSpec((tm, LANE), row(OFF_KR // LANE)),
            pl.BlockSpec((tm, LANE), row(OFF_KRS // LANE)),
            pl.BlockSpec((tm, LANE), row(0)),
            pl.BlockSpec((tm, LANE), row(0)),
            pl.BlockSpec((1, MLA_Q_RANK), const),
            pl.BlockSpec((1, MLA_KV_RANK), const),
            pl.BlockSpec((MLA_Q_RANK, HP), const),
            pl.BlockSpec((MLA_Q_RANK, HP), const),
            pl.BlockSpec((MLA_KV_RANK, HP), const),
            pl.BlockSpec((MLA_KV_RANK, MLA_HEADS * MLA_VDIM), const),
        ],
        out_specs=[
            pl.BlockSpec((tm, HP), row(0)),
            pl.BlockSpec((tm, HP), row(0)),
            pl.BlockSpec((tm, MLA_HEADS * MLA_VDIM), row(0)),
        ],
        out_shape=[
            jax.ShapeDtypeStruct((T, HP), BF16),
            jax.ShapeDtypeStruct((T, HP), BF16),
            jax.ShapeDtypeStruct((T, MLA_HEADS * MLA_VDIM), BF16),
        ],
        compiler_params=_cparams(("arbitrary",)),
        name="mla_prep",
    )(p, p, p, p, ctab, stab, qg, kvg, wq, wqr, wk, wv)


def _attn_kernel(q_ref, k_ref, v_ref, o_ref, *, tq, tk):
    qi = pl.program_id(2)
    n_full = (qi * tq) // tk
    n_diag = tq // tk
    q_pos = qi * tq + lax.broadcasted_iota(jnp.int32, (tq, tk), 0)
    k_off = lax.broadcasted_iota(jnp.int32, (tq, tk), 1)
    lane = lax.broadcasted_iota(jnp.int32, (tq, LANE), 1)
    outs = []
    for hh in range(2):
        hsl = slice(hh * HEAD_PAD, (hh + 1) * HEAD_PAD)
        q = q_ref[:, hsl]

        def step(ki, carry, masked):
            m, l, acc = carry
            start = pl.multiple_of(ki * tk, tk)
            k = k_ref[pl.ds(start, tk), hsl]
            s = lax.dot_general(q, k, (((1,), (1,)), ((), ())), preferred_element_type=F32)
            if masked:
                allowed = ((start + k_off) // CHUNK) <= (q_pos // CHUNK)
                s = jnp.where(allowed, s, -1e30)
            m_new = jnp.maximum(m, jnp.max(s, axis=-1, keepdims=True))
            pr = jnp.exp(s - m_new)
            alpha = jnp.exp(m - m_new)
            l_new = alpha * l + jnp.sum(pr, axis=-1, keepdims=True)
            v = v_ref[pl.ds(start, tk), :]
            acc_new = alpha * acc + _bdot(pr.astype(BF16), v)
            return m_new, l_new, acc_new

        carry = (jnp.full((tq, 1), -1e30, F32), jnp.zeros((tq, 1), F32), jnp.zeros((tq, LANE), F32))
        carry = lax.fori_loop(0, n_full, functools.partial(step, masked=False), carry)
        for d in range(n_diag):
            carry = step(n_full + d, carry, True)
        m, l, acc = carry
        outs.append(acc / l)
    o_ref[...] = jnp.where(lane < MLA_VDIM, outs[0], outs[1])


def _attention(q, k, v, batch, seq):
    T = q.shape[0]
    tq, tk = 512, 512
    nq = seq // tq
    npair = MLA_HEADS // 2
    return pl.pallas_call(
        functools.partial(_attn_kernel, tq=tq, tk=tk),
        grid=(batch, npair, nq),
        in_specs=[
            pl.BlockSpec((tq, 2 * HEAD_PAD), lambda b, h, i: (b * nq + i, h)),
            pl.BlockSpec((seq, 2 * HEAD_PAD), lambda b, h, i: (b, h)),
            pl.BlockSpec((seq, 2 * MLA_VDIM), lambda b, h, i: (b, h)),
        ],
        out_specs=pl.BlockSpec((tq, 2 * MLA_VDIM), lambda b, h, i: (b * nq + i, h)),
        out_shape=jax.ShapeDtypeStruct((T, MLA_HEADS * MLA_VDIM), F32),
        compiler_params=_cparams(("arbitrary", "arbitrary", "arbitrary")),
        name="attention",
    )(q, k, v)


def _gmlp_kernel(p_ref, g_ref, b_ref, w_ref, bs_ref, o_ref, *, tm):
    x = p_ref[...]
    ge = 0.5 * x * (1.0 + lax.erf(x * (1.0 / math.sqrt(2.0))))
    u = ge[:, :GM_WIDTH]
    v = ge[:, GM_WIDTH:]
    mu = jnp.mean(v, axis=-1, keepdims=True)
    vc = v - mu
    var = jnp.mean(vc * vc, axis=-1, keepdims=True)
    vn = (vc * lax.rsqrt(var + LN_EPS) * g_ref[...] + b_ref[...]).astype(BF16)
    row = lax.broadcasted_iota(jnp.int32, (GM_BLOCK, GM_BLOCK), 0)
    col = lax.broadcasted_iota(jnp.int32, (GM_BLOCK, GM_BLOCK), 1)
    mask = (col // CHUNK) <= (row // CHUNK)
    for g in range(GM_GROUPS):
        w = jnp.where(mask, w_ref[g], 0.0).astype(BF16)
        csl = slice(g * GM_GROUP_CH, (g + 1) * GM_GROUP_CH)
        for blk in range(tm // GM_BLOCK):
            rsl = slice(blk * GM_BLOCK, (blk + 1) * GM_BLOCK)
            s = _bdot(w, vn[rsl, csl]) + bs_ref[g]
            o_ref[rsl, csl] = u[rsl, csl] * s


def _gmlp(p, ln_g, ln_b, w_s, b_s):
    T = p.shape[0]
    tm = 512
    return pl.pallas_call(
        functools.partial(_gmlp_kernel, tm=tm),
        grid=(T // tm,),
        in_specs=[
            pl.BlockSpec((tm, 2 * GM_WIDTH), lambda i: (i, OFF_GM // (2 * GM_WIDTH))),
            pl.BlockSpec((1, GM_WIDTH), lambda i: (0, 0)),
            pl.BlockSpec((1, GM_WIDTH), lambda i: (0, 0)),
            pl.BlockSpec((GM_GROUPS, GM_BLOCK, GM_BLOCK), lambda i: (0, 0, 0)),
            pl.BlockSpec((GM_GROUPS, GM_BLOCK, 1), lambda i: (0, 0, 0)),
        ],
        out_specs=pl.BlockSpec((tm, GM_WIDTH), lambda i: (i, 0)),
        out_shape=jax.ShapeDtypeStruct((T, GM_WIDTH), F32),
        compiler_params=_cparams(("arbitrary",)),
        name="gmlp",
    )(p, ln_g, ln_b, w_s, b_s.reshape(GM_GROUPS, GM_BLOCK, 1))


def _split3_dot(x, m_bf16):
    x1 = x.astype(BF16)
    r1 = x - x1.astype(F32)
    x2 = r1.astype(BF16)
    x3 = (r1 - x2.astype(F32)).astype(BF16)
    return _bdot(x1, m_bf16) + _bdot(x2, m_bf16) + _bdot(x3, m_bf16)


def _rw_prep_kernel(*refs, tm, has_vmix):
    if has_vmix:
        (r_ref, k_ref, v_ref, lat_ref, vf_ref, mur_ref, muk_ref, muv_ref, mul_ref, w0_ref, w2_ref, a0_ref, a2_ref,
         kk_ref, ka_ref, v0_ref, v1_ref, v2_ref, ones_ref,
         ro_ref, lwo_ref, ko_ref, vo_ref, ao_ref, bo_ref, cr_ref, ck_ref, cv_ref, cl_ref) = refs
    else:
        (r_ref, k_ref, v_ref, lat_ref, mur_ref, muk_ref, muv_ref, mul_ref, w0_ref, w2_ref, a0_ref, a2_ref,
         kk_ref, ka_ref, ones_ref,
         ro_ref, lwo_ref, ko_ref, vo_ref, ao_ref, bo_ref, cr_ref, ck_ref, cv_ref, cl_ref) = refs

    @pl.when(pl.program_id(1) == 0)
    def _():
        for c in (cr_ref, ck_ref, cv_ref, cl_ref):
            c[...] = jnp.zeros_like(c)

    def shifted(x_ref, carry_ref, mu_ref):
        x = x_ref[...]
        rolled = pltpu.roll(x, 1, 0)
        rid = lax.broadcasted_iota(jnp.int32, x.shape, 0)
        prev = jnp.where(rid == 0, carry_ref[0:1, :], rolled)
        carry_ref[0:1, :] = x[tm - 1:tm, :]
        return x + (prev - x) * mu_ref[...]

    r = shifted(r_ref, cr_ref, mur_ref)
    k = shifted(k_ref, ck_ref, muk_ref)
    v = shifted(v_ref, cv_ref, muv_ref)
    lat = shifted(lat_ref, cl_ref, mul_ref)

    ww = w0_ref[...] + _mm(jnp.tanh(lat), w2_ref[...])
    nw = -ww
    softplus = jnp.maximum(nw, 0.0) + jnp.log(1.0 + jnp.exp(-jnp.abs(nw)))
    w_log = -softplus - 0.5
    lwo_ref[...] = -jnp.exp(w_log)
    a = _sigmoid(a0_ref[...] + _mm(lat, a2_ref[...]))
    if has_vmix:
        gate = _sigmoid(v0_ref[...] + _mm(_mm(v, v1_ref[...]), v2_ref[...]))
        v = v + (vf_ref[...] - v) * gate
    kk = k * kk_ref[...]
    n2 = _split3_dot(kk * kk, ones_ref[...])
    kk = kk / jnp.maximum(jnp.sqrt(n2), 1e-12)
    ro_ref[...] = r
    ko_ref[...] = k * (1.0 + (a - 1.0) * ka_ref[...])
    vo_ref[...] = v
    ao_ref[...] = -kk
    bo_ref[...] = kk * a


def _rw_prep(p, v_first, prm, batch, seq):
    T = p.shape[0]
    tm = 512
    nt = seq // tm
    has_vmix = v_first is not None
    W = RW_WIDTH
    rowp = lambda blk: (lambda b, j: (b * nt + j, blk))
    row0 = lambda b, j: (b * nt + j, 0)
    const = lambda b, j: (0, 0)
    vec = pl.BlockSpec((1, W), const)
    in_specs = [
        pl.BlockSpec((tm, W), rowp(OFF_R // W)),
        pl.BlockSpec((tm, W), rowp(OFF_K // W)),
        pl.BlockSpec((tm, W), rowp(OFF_V // W)),
        pl.BlockSpec((tm, LANE), rowp(OFF_LAT // LANE)),
    ]
    args = [p, p, p, p]
    if has_vmix:
        in_specs.append(pl.BlockSpec((tm, W), row0))
        args.append(v_first)
    in_specs += [vec, vec, vec, pl.BlockSpec((1, LANE), const), vec, pl.BlockSpec((LANE, W), const),
                 vec, pl.BlockSpec((LANE, W), const), vec, vec]
    args += [prm["mu_r"], prm["mu_k"], prm["mu_v"], prm["mu_l"], prm["w0"], prm["w2"], prm["a0"], prm["a2"],
             prm["k_k"], prm["k_a"]]
    if has_vmix:
        in_specs += [vec, pl.BlockSpec((W, LANE), const), pl.BlockSpec((LANE, W), const)]
        args += [prm["v0"], prm["v1"], prm["v2"]]
    in_specs.append(pl.BlockSpec((W, W), const))
    args.append(prm["head_ones"])
    out = pl.BlockSpec((tm, W), row0)
    return pl.pallas_call(
        functools.partial(_rw_prep_kernel, tm=tm, has_vmix=has_vmix),
        grid=(batch, nt),
        in_specs=in_specs,
        out_specs=[out] * 6,
        out_shape=[jax.ShapeDtypeStruct((T, W), F32)] * 6,
        scratch_shapes=[pltpu.VMEM((8, W), F32)] * 3 + [pltpu.VMEM((8, LANE), F32)],
        compiler_params=_cparams(("arbitrary", "arbitrary")),
        name="rw_prep",
    )(*args)


def _unit_lower_inverse(lows, row, col, n):
    eye = jnp.where(row == col, 1.0, 0.0)
    base = 16
    same = jnp.where((row // base) == (col // base), 1.0, 0.0)
    pws = [low * same for low in lows]
    invs = [eye + pw for pw in pws]
    span = 2
    while span < base:
        pws = [_pdot(pw, pw) for pw in pws]
        invs = [inv + _pdot(inv, pw) for inv, pw in zip(invs, pws)]
        span *= 2
    size = base
    while size < n:
        pair = (row // (2 * size)) == (col // (2 * size))
        sel = jnp.where(pair, jnp.where((row // size) == (col // size), 0.0, 1.0), 0.0)
        tmp = [_pdot(inv, low * sel) for inv, low in zip(invs, lows)]
        invs = [inv + _pdot(t, inv) for inv, t in zip(invs, tmp)]
        size *= 2
    return invs


def _wkv_kernel(r_ref, lw_ref, k_ref, v_ref, a_ref, b_ref, rk_ref, g_ref, beta_ref, ones_ref, o_ref, s_ref, *, C):
    @pl.when(pl.program_id(1) == 0)
    def _():
        s_ref[...] = jnp.zeros_like(s_ref)

    P = 2 * C
    r = r_ref[...]
    lw = lw_ref[...]
    k = k_ref[...]
    v = v_ref[...]
    a_s = a_ref[...]
    b_s = b_ref[...]
    rowc = lax.broadcasted_iota(jnp.int32, (C, C), 0)
    colc = lax.broadcasted_iota(jnp.int32, (C, C), 1)
    tri = jnp.where(rowc >= colc, 1.0, 0.0).astype(BF16)
    l1 = lw.astype(BF16)
    d1 = lw - l1.astype(F32)
    l2 = d1.astype(BF16)
    l3 = (d1 - l2.astype(F32)).astype(BF16)
    cum = _bdot(tri, l1) + _bdot(tri, l2) + _bdot(tri, l3)
    cum_last = cum[C - 1:C, :]
    e_in = jnp.exp(cum)
    e_inv = jnp.exp(-cum)
    e_end = jnp.exp(cum_last - cum)
    a_t = a_s * jnp.exp(cum - lw)
    r_t = r * e_in
    b_t = b_s * e_inv
    k_t = k * e_inv
    b_e = b_s * e_end
    k_e = k * e_end
    p_end = jnp.exp(cum_last)
    rk = r * k * rk_ref[...]

    row = lax.broadcasted_iota(jnp.int32, (P, P), 0)
    col = lax.broadcasted_iota(jnp.int32, (P, P), 1)
    same_head = (row // C) == (col // C)
    strict = jnp.where(same_head, jnp.where((row % C) > (col % C), 1.0, 0.0), 0.0)
    incl = jnp.where(same_head, jnp.where((row % C) >= (col % C), 1.0, 0.0), 0.0)
    even = lax.broadcasted_iota(jnp.int32, (C, LANE), 1) < RW_HEAD
    ones = ones_ref[...]

    def stack(x):
        return jnp.concatenate([jnp.where(even, x, 0.0), jnp.where(even, 0.0, x)], axis=0)

    nt = (((1,), (1,)), ((), ()))
    tn = (((0,), (0,)), ((), ()))
    pairs = range(RW_HEADS // 2)
    sls = [slice(pr * LANE, (pr + 1) * LANE) for pr in pairs]
    incl2 = jnp.concatenate([incl, incl], axis=1)
    ars = [jnp.concatenate([stack(a_t[:, sl]), stack(r_t[:, sl])], axis=0) for sl in sls]
    bks = [jnp.concatenate([stack(b_t[:, sl]), stack(k_t[:, sl])], axis=0) for sl in sls]
    v_ps = [stack(v[:, sl]) for sl in sls]
    gs = [_pdot(ar, bk, nt) for ar, bk in zip(ars, bks)]
    s0s = [s_ref[pr] for pr in pairs]
    sas = [_pdot(ar, s0, nt) for ar, s0 in zip(ars, s0s)]
    tinvs = _unit_lower_inverse([g[:P, :P] * strict for g in gs], row, col, C)
    akvs = [_pdot(g[:P, P:] * strict, vp) for g, vp in zip(gs, v_ps)]
    us = [_pdot(tinv, sa[:P] + akv) for tinv, sa, akv in zip(tinvs, sas, akvs)]
    uvs = [jnp.concatenate([u, vp], axis=0) for u, vp in zip(us, v_ps)]
    bkes = [jnp.concatenate([stack(b_e[:, sl]), stack(k_e[:, sl])], axis=0) for sl in sls]
    for pr, sl, s0, uv, bke in zip(pairs, sls, s0s, uvs, bkes):
        s_ref[pr] = s0 * p_end[:, sl] + _pdot(uv, bke, tn)
    y_pss = [sa[P:] + _pdot(g[P:, :] * incl2, uv) for sa, g, uv in zip(sas, gs, uvs)]
    ys = [y_ps[:C] + y_ps[C:] for y_ps in y_pss]
    mus = [_split3_dot(y, ones) * (1.0 / RW_HEAD) for y in ys]
    ycs = [y - mu for y, mu in zip(ys, mus)]
    vrs = [_split3_dot(yc * yc, ones) * (1.0 / RW_HEAD) for yc in ycs]
    bns = [_split3_dot(rk[:, sl], ones) * v[:, sl] for sl in sls]
    for sl, yc, var, bonus in zip(sls, ycs, vrs, bns):
        o_ref[:, sl] = yc * lax.rsqrt(var + RW_LN_EPS) * g_ref[:, sl] + beta_ref[:, sl] + bonus


def _wkv(r, lw, k, v, a_s, b_s, r_k, lnx_g, lnx_b, pair_ones, batch, seq):
    T = r.shape[0]
    C = 64
    nc = seq // C
    W = RW_WIDTH
    row = pl.BlockSpec((C, W), lambda b, c: (b * nc + c, 0))
    vec = pl.BlockSpec((1, W), lambda b, c: (0, 0))
    return pl.pallas_call(
        functools.partial(_wkv_kernel, C=C),
        grid=(batch, nc),
        in_specs=[row] * 6 + [vec] * 3 + [pl.BlockSpec((LANE, LANE), lambda b, c: (0, 0))],
        out_specs=row,
        out_shape=jax.ShapeDtypeStruct((T, W), F32),
        scratch_shapes=[pltpu.VMEM((RW_HEADS // 2, LANE, LANE), F32)],
        compiler_params=_cparams(("arbitrary", "arbitrary")),
        name="wkv7",
    )(r, lw, k, v, a_s, b_s, r_k, lnx_g, lnx_b, pair_ones)


def _merge_kernel(ya_ref, yg_ref, yr_ref, za_ref, zg_ref, zr_ref, ga_ref, gg_ref, gr_ref,
                  x_ref, gate_ref, pg_ref, wbr_ref, wout_ref, o_ref):
    acc = None
    for n, (y_ref, z_ref, g_ref) in enumerate(((ya_ref, za_ref, ga_ref), (yg_ref, zg_ref, gg_ref),
                                                (yr_ref, zr_ref, gr_ref))):
        z = z_ref[...]
        br = (y_ref[...] * (z * _sigmoid(z))).astype(BF16)
        pr = _bdot(br, wbr_ref[n]) * _sigmoid(g_ref[...])
        acc = pr if acc is None else acc + pr
    y = _bdot(acc.astype(BF16), wout_ref[...])
    yn = y * lax.rsqrt(jnp.mean(y * y, axis=-1, keepdims=True) + EPS) * pg_ref[...]
    o_ref[...] = x_ref[...] + gate_ref[0] * yn


def _merge(y_mla, y_gm, y_rw, p, x2d, gate, post_g, w_br, w_out, seq):
    T, D = x2d.shape
    tm = 256
    per_b = seq // tm
    rowp = lambda blk: (lambda i: (i, blk))
    yspec = pl.BlockSpec((tm, BW), rowp(0))
    return pl.pallas_call(
        _merge_kernel,
        grid=(T // tm,),
        in_specs=[yspec, yspec, yspec]
        + [pl.BlockSpec((tm, BW), rowp(OFF_Z // BW + n)) for n in range(3)]
        + [pl.BlockSpec((tm, D), rowp(OFF_G // D + n)) for n in range(3)]
        + [
            pl.BlockSpec((tm, D), rowp(0)),
            pl.BlockSpec((1, 1, D), lambda i: (i // per_b, 0, 0)),
            pl.BlockSpec((1, D), lambda i: (0, 0)),
            pl.BlockSpec((3, BW, D), lambda i: (0, 0, 0)),
            pl.BlockSpec((D, D), lambda i: (0, 0)),
        ],
        out_specs=pl.BlockSpec((tm, D), rowp(0)),
        out_shape=jax.ShapeDtypeStruct((T, D), F32),
        compiler_params=_cparams(("arbitrary",)),
        name="merge",
    )(y_mla, y_gm, y_rw, p, p, p, p, p, p, x2d, gate, post_g, w_br, w_out)


def _relayout_w_in(w_in):
    L, D, _ = w_in.shape
    o_q, o_kv, o_kr, o_gm = 0, MLA_Q_RANK, MLA_Q_RANK + MLA_KV_RANK, MLA_Q_RANK + MLA_KV_RANK + MLA_ROPE
    o_rw = o_gm + 2 * GM_WIDTH
    o_lat = o_rw + 3 * RW_WIDTH
    o_z = o_lat + 2 * RW_LORA
    o_g = o_z + 3 * BW
    half = MLA_ROPE // 2
    kr = w_in[:, :, o_kr:o_kr + MLA_ROPE]
    krs = jnp.concatenate([kr[:, :, half:], kr[:, :, :half]], axis=-1)
    z64 = jnp.zeros((L, D, MLA_NOPE), w_in.dtype)
    z32 = jnp.zeros((L, D, LANE - MLA_NOPE - MLA_ROPE), w_in.dtype)
    pad = jnp.zeros((L, D, NP - OFF_LAT - LANE), w_in.dtype)
    cols = [
        w_in[:, :, o_gm:o_lat],
        w_in[:, :, o_z:o_g],
        w_in[:, :, o_g:],
        w_in[:, :, o_q:o_kr],
        z64, kr, z32,
        z64, krs, z32,
        w_in[:, :, o_lat:o_z],
        pad,
    ]
    return jnp.concatenate(cols, axis=-1).astype(BF16)


def _relayout_mla(w_uq, w_ukv):
    L = w_uq.shape[0]
    dq = MLA_NOPE + MLA_ROPE
    half = MLA_ROPE // 2
    wq = w_uq.reshape(L, MLA_Q_RANK, MLA_HEADS, dq)
    padq = ((0, 0), (0, 0), (0, 0), (0, HEAD_PAD - dq))
    wq_p = jnp.pad(wq, padq).reshape(L, MLA_Q_RANK, MLA_HEADS * HEAD_PAD)
    rope = wq[..., MLA_NOPE:]
    rot = jnp.concatenate([jnp.zeros_like(wq[..., :MLA_NOPE]), rope[..., half:], rope[..., :half]], axis=-1)
    wqr_p = jnp.pad(rot, padq).reshape(L, MLA_Q_RANK, MLA_HEADS * HEAD_PAD)
    wkv = w_ukv.reshape(L, MLA_KV_RANK, MLA_HEADS, MLA_NOPE + MLA_VDIM)
    wk_p = jnp.pad(wkv[..., :MLA_NOPE], ((0, 0), (0, 0), (0, 0), (0, HEAD_PAD - MLA_NOPE)))
    wk_p = wk_p.reshape(L, MLA_KV_RANK, MLA_HEADS * HEAD_PAD)
    wv = wkv[..., MLA_NOPE:].reshape(L, MLA_KV_RANK, MLA_HEADS * MLA_VDIM)
    return wq_p.astype(BF16), wqr_p.astype(BF16), wk_p.astype(BF16), wv.astype(BF16)


def kernel(x, c, positions, pre_g, post_g, w_ada, b_ada, w_in, mla_q_norm, mla_w_uq, mla_kv_norm, mla_w_ukv,
           gm_ln_g, gm_ln_b, gm_w_s, gm_b_s, rw_mu, rw_w0, rw_w2, rw_a0, rw_a2, rw_k_k, rw_k_a, rw_r_k,
           rw_lnx_g, rw_lnx_b, rw_v0, rw_v1, rw_v2, w_br, w_out):
    B, S, D = x.shape
    L = w_in.shape[0]
    T = B * S
    x2d = x.reshape(T, D)

    c_pad = jnp.pad(c, ((0, 8 - B), (0, 0)))
    mods = _ada(c_pad, w_ada, b_ada)[:, :B]
    ctab, stab = _rope_tables(positions)

    w_in_p = _relayout_w_in(w_in)
    wq_p, wqr_p, wk_p, wv_p = _relayout_mla(mla_w_uq, mla_w_ukv)
    w_br_b = w_br.astype(BF16)
    w_out_b = w_out.astype(BF16)
    zl = jnp.zeros((RW_LORA, RW_WIDTH), F32)
    hid = np.arange(RW_WIDTH) // RW_HEAD
    head_ones = jnp.asarray((hid[:, None] == hid[None, :]).astype(np.float32)).astype(BF16)

    v_first = None
    for l in range(L):
        shift = mods[l, :, :D].reshape(B, 1, D)
        scale = mods[l, :, D:2 * D].reshape(B, 1, D)
        gate = mods[l, :, 2 * D:].reshape(B, 1, D)
        p = _inproj(x2d, scale, shift, pre_g[l].reshape(1, D), w_in_p[l], S)

        q, k, v = _mla_prep(p, ctab, stab, mla_q_norm[l].reshape(1, -1), mla_kv_norm[l].reshape(1, -1),
                            wq_p[l], wqr_p[l], wk_p[l], wv_p[l])
        y_mla = _attention(q, k, v, B, S)

        y_gm = _gmlp(p, gm_ln_g[l].reshape(1, -1), gm_ln_b[l].reshape(1, -1), gm_w_s[l], gm_b_s[l])

        mu = rw_mu[l]
        prm = {
            "mu_r": mu[:RW_WIDTH].reshape(1, -1),
            "mu_k": mu[RW_WIDTH:2 * RW_WIDTH].reshape(1, -1),
            "mu_v": mu[2 * RW_WIDTH:3 * RW_WIDTH].reshape(1, -1),
            "mu_l": mu[3 * RW_WIDTH:].reshape(1, -1),
            "w0": rw_w0[l].reshape(1, -1),
            "w2": jnp.concatenate([rw_w2[l], zl], axis=0),
            "a0": rw_a0[l].reshape(1, -1),
            "a2": jnp.concatenate([zl, rw_a2[l]], axis=0),
            "k_k": rw_k_k[l].reshape(1, -1),
            "k_a": rw_k_a[l].reshape(1, -1),
            "head_ones": head_ones,
        }
        if l > 0:
            prm["v0"] = rw_v0[l - 1].reshape(1, -1)
            prm["v1"] = jnp.pad(rw_v1[l - 1], ((0, 0), (0, LANE - RW_V_LORA)))
            prm["v2"] = jnp.pad(rw_v2[l - 1], ((0, LANE - RW_V_LORA), (0, 0)))
        r_s, lw_s, k_s, v_s, a_s, b_s = _rw_prep(p, v_first, prm, B, S)
        if l == 0:
            v_first = v_s
        y_rw = _wkv(r_s, lw_s, k_s, v_s, a_s, b_s, rw_r_k[l].reshape(1, -1), rw_lnx_g[l].reshape(1, -1),
                    rw_lnx_b[l].reshape(1, -1), head_ones[:LANE, :LANE], B, S)

        x2d = _merge(y_mla, y_gm, y_rw, p, x2d, gate, post_g[l].reshape(1, D), w_br_b[l], w_out_b[l], S)
    return x2d.reshape(B, S, D)
```

```python
import functools
import math

import jax
import jax.numpy as jnp
import numpy as np
from jax import lax
from jax.experimental import pallas as pl
from jax.experimental.pallas import tpu as pltpu

F32 = jnp.float32
BF16 = jnp.bfloat16
HIGHEST = lax.Precision.HIGHEST

CHUNK = 64
EPS = 1e-6
LN_EPS = 1e-5
MLA_HEADS = 8
MLA_Q_RANK = 256
MLA_KV_RANK = 128
MLA_NOPE = 64
MLA_ROPE = 32
MLA_VDIM = 64
ROPE_THETA = 10000.0
GM_GROUPS = 4
GM_GROUP_CH = 128
GM_WIDTH = 512
GM_BLOCK = 128
RW_HEADS = 8
RW_HEAD = 64
RW_WIDTH = 512
RW_LORA = 64
RW_V_LORA = 32
RW_LN_EPS = 64e-5
BW = 512
LANE = 128
HEAD_PAD = 128

OFF_GM = 0
OFF_R, OFF_K, OFF_V = 1024, 1536, 2048
OFF_Z = 2560
OFF_G = 4096
OFF_QLAT = 7168
OFF_KVLAT = 7424
OFF_KR = 7552
OFF_KRS = 7680
OFF_LAT = 7808
NP = 8192

VMEM_LIMIT = 56 * 1024 * 1024
NT = (((1,), (1,)), ((), ()))
TN = (((0,), (0,)), ((), ()))


def _cparams(sem):
    return pltpu.CompilerParams(dimension_semantics=sem, vmem_limit_bytes=VMEM_LIMIT)


def _mm(a, b):
    return jnp.dot(a, b, preferred_element_type=F32, precision=HIGHEST)


def _bdot(a, b, dims=(((1,), (0,)), ((), ()))):
    return lax.dot_general(a.astype(BF16), b.astype(BF16), dims, preferred_element_type=F32)


def _sigmoid(x):
    return 1.0 / (1.0 + jnp.exp(-x))


def _split3_dot(x, m_bf16):
    x1 = x.astype(BF16)
    r1 = x - x1.astype(F32)
    x2 = r1.astype(BF16)
    x3 = (r1 - x2.astype(F32)).astype(BF16)
    return _bdot(x1, m_bf16) + _bdot(x2, m_bf16) + _bdot(x3, m_bf16)


def _ada_kernel(c_ref, w_ref, b_ref, o_ref):
    c = c_ref[...]
    ca = c * _sigmoid(c)
    o_ref[0] = _mm(ca, w_ref[0]) + b_ref[0]


def _ada(c_pad, w_ada, b_ada):
    L, D, D3 = w_ada.shape
    tn = 1024
    return pl.pallas_call(
        _ada_kernel,
        grid=(L, D3 // tn),
        in_specs=[
            pl.BlockSpec((8, D), lambda l, j: (0, 0)),
            pl.BlockSpec((1, D, tn), lambda l, j: (l, 0, j)),
            pl.BlockSpec((1, 1, tn), lambda l, j: (l, 0, j)),
        ],
        out_specs=pl.BlockSpec((1, 8, tn), lambda l, j: (l, 0, j)),
        out_shape=jax.ShapeDtypeStruct((L, 8, D3), F32),
        compiler_params=_cparams(("arbitrary", "arbitrary")),
        name="ada",
    )(c_pad, w_ada, b_ada.reshape(L, 1, D3))


def _rope_kernel(pos_ref, freq_ref, sign_ref, c_ref, s_ref):
    pos = pos_ref[...].astype(F32)
    ang = pos * freq_ref[...]
    lane = lax.broadcasted_iota(jnp.int32, ang.shape, 1)
    is_rope = (lane >= MLA_NOPE) & (lane < MLA_NOPE + MLA_ROPE)
    c_ref[...] = jnp.where(is_rope, jnp.cos(ang), jnp.where(lane < MLA_NOPE, 1.0, 0.0))
    s_ref[...] = jnp.sin(ang) * sign_ref[...]


def _rope_tables(positions):
    T = positions.size
    tm = 2048
    inv_freq = ROPE_THETA ** (-np.arange(0, MLA_ROPE, 2, dtype=np.float32) / MLA_ROPE)
    half = MLA_ROPE // 2
    freq = np.zeros((1, LANE), np.float32)
    freq[0, MLA_NOPE:MLA_NOPE + half] = inv_freq
    freq[0, MLA_NOPE + half:MLA_NOPE + MLA_ROPE] = inv_freq
    sign = np.zeros((1, LANE), np.float32)
    sign[0, MLA_NOPE:MLA_NOPE + half] = -1.0
    sign[0, MLA_NOPE + half:MLA_NOPE + MLA_ROPE] = 1.0
    return pl.pallas_call(
        _rope_kernel,
        grid=(T // tm,),
        in_specs=[
            pl.BlockSpec((tm, 1), lambda i: (i, 0)),
            pl.BlockSpec((1, LANE), lambda i: (0, 0)),
            pl.BlockSpec((1, LANE), lambda i: (0, 0)),
        ],
        out_specs=[pl.BlockSpec((tm, LANE), lambda i: (i, 0))] * 2,
        out_shape=[jax.ShapeDtypeStruct((T, LANE), F32)] * 2,
        compiler_params=_cparams(("arbitrary",)),
        name="rope_tables",
    )(positions.reshape(T, 1), jnp.asarray(freq), jnp.asarray(sign))


def _inproj_kernel(x_ref, sc_ref, sh_ref, g_ref, w_ref, o_ref, h_ref):
    @pl.when(pl.program_id(1) == 0)
    def _():
        x = x_ref[...]
        ms = jnp.mean(x * x, axis=-1, keepdims=True)
        y = x * lax.rsqrt(ms + EPS) * g_ref[...]
        h_ref[...] = (y * (1.0 + sc_ref[0]) + sh_ref[0]).astype(BF16)

    o_ref[...] = _bdot(h_ref[...], w_ref[...]).astype(BF16)


def _inproj(x2d, scale, shift, pre_g, w_in_p, layer, seq):
    T, D = x2d.shape
    tm, tn = 1024, 1024
    per_b = seq // tm
    return pl.pallas_call(
        _inproj_kernel,
        grid=(T // tm, NP // tn),
        in_specs=[
            pl.BlockSpec((tm, D), lambda i, j: (i, 0)),
            pl.BlockSpec((1, 1, D), lambda i, j: (i // per_b, 0, 0)),
            pl.BlockSpec((1, 1, D), lambda i, j: (i // per_b, 0, 0)),
            pl.BlockSpec((1, D), lambda i, j: (0, 0)),
            pl.BlockSpec((None, D, tn), lambda i, j: (layer, 0, j)),
        ],
        out_specs=pl.BlockSpec((tm, tn), lambda i, j: (i, j)),
        out_shape=jax.ShapeDtypeStruct((T, NP), BF16),
        scratch_shapes=[pltpu.VMEM((tm, D), BF16)],
        compiler_params=_cparams(("arbitrary", "arbitrary")),
        name="inproj",
    )(x2d, scale, shift, pre_g, w_in_p)


def _mla_prep_kernel(ql_ref, kvl_ref, kr_ref, krs_ref, ct_ref, st_ref, qg_ref, kvg_ref,
                     wq_ref, wqr_ref, wk_ref, wv_ref, vone_ref, q_ref, k_ref, v_ref, *, scale):
    ql = ql_ref[...].astype(F32)
    qn = (ql * lax.rsqrt(jnp.mean(ql * ql, axis=-1, keepdims=True) + EPS) * qg_ref[...]).astype(BF16)
    kvl = kvl_ref[...].astype(F32)
    kvn = (kvl * lax.rsqrt(jnp.mean(kvl * kvl, axis=-1, keepdims=True) + EPS) * kvg_ref[...]).astype(BF16)
    ct = ct_ref[...]
    st = st_ref[...]
    kpe = kr_ref[...].astype(F32) * ct + krs_ref[...].astype(F32) * st
    v_ref[...] = (_bdot(kvn, wv_ref[...]) + vone_ref[...]).astype(BF16)
    for h in range(MLA_HEADS):
        sl = slice(h * HEAD_PAD, (h + 1) * HEAD_PAD)
        q = _bdot(qn, wq_ref[:, sl]) * ct + _bdot(qn, wqr_ref[:, sl]) * st
        q_ref[:, sl] = (q * scale).astype(BF16)
        k_ref[:, sl] = (_bdot(kvn, wk_ref[:, sl]) + kpe).astype(BF16)


def _mla_prep(p, ctab, stab, qg, kvg, wq, wqr, wk, wv, vone, layer):
    T = p.shape[0]
    tm = 512
    HP = MLA_HEADS * HEAD_PAD
    scale = float((MLA_NOPE + MLA_ROPE) ** -0.5) * math.log2(math.e)
    row = lambda blk: (lambda i: (i, blk))
    const = lambda i: (0, 0)
    wspec = lambda rows: pl.BlockSpec((None, rows, HP), lambda i: (layer, 0, 0))
    return pl.pallas_call(
        functools.partial(_mla_prep_kernel, scale=scale),
        grid=(T // tm,),
        in_specs=[
            pl.BlockSpec((tm, MLA_Q_RANK), row(OFF_QLAT // MLA_Q_RANK)),
            pl.BlockSpec((tm, LANE), row(OFF_KVLAT // LANE)),
            pl.BlockSpec((tm, LANE), row(OFF_KR // LANE)),
            pl.BlockSpec((tm, LANE), row(OFF_KRS // LANE)),
            pl.BlockSpec((tm, LANE), row(0)),
            pl.BlockSpec((tm, LANE), row(0)),
            pl.BlockSpec((1, MLA_Q_RANK), const),
            pl.BlockSpec((1, MLA_KV_RANK), const),
            wspec(MLA_Q_RANK),
            wspec(MLA_Q_RANK),
            wspec(MLA_KV_RANK),
            wspec(MLA_KV_RANK),
            pl.BlockSpec((1, HP), const),
        ],
        out_specs=[pl.BlockSpec((tm, HP), row(0))] * 3,
        out_shape=[jax.ShapeDtypeStruct((T, HP), BF16)] * 3,
        compiler_params=_cparams(("arbitrary",)),
        name="mla_prep",
    )(p, p, p, p, ctab, stab, qg, kvg, wq, wqr, wk, wv, vone)


def _attn_kernel(q_ref, k_ref, v_ref, o_ref, s_ref, p_ref, m_ref, al_ref, acc_ref, *, tq):
    qi = pl.program_id(2)
    m_ref[...] = jnp.full(m_ref.shape, -1e30, F32)
    acc_ref[...] = jnp.zeros(acc_ref.shape, F32)
    col = lax.broadcasted_iota(jnp.int32, (CHUNK, tq), 1)
    strips = [slice(r * CHUNK, (r + 1) * CHUNK) for r in range(tq // CHUNK)]

    def kv_step(ki, masked):
        start = pl.multiple_of(ki * tq, tq)
        for hh in range(2):
            hsl = slice(hh * HEAD_PAD, (hh + 1) * HEAD_PAD)
            s_ref[hh] = _bdot(q_ref[:, hsl], k_ref[pl.ds(start, tq), hsl], NT)
        if masked:
            for hh in range(2):
                for r, rows in enumerate(strips):
                    s_ref[hh, rows, :] = jnp.where(col < (r + 1) * CHUNK, s_ref[hh, rows, :], -1e30)
        for hh in range(2):
            m_old = m_ref[hh]
            m_new = jnp.maximum(m_old, jnp.max(s_ref[hh], axis=-1, keepdims=True))
            al_ref[hh] = jnp.exp2(m_old - m_new)
            m_ref[hh] = m_new
        for hh in range(2):
            m_new = m_ref[hh]
            for j in range(tq // LANE):
                csl = slice(j * LANE, (j + 1) * LANE)
                p_ref[hh, :, csl] = jnp.exp2(s_ref[hh, :, csl] - m_new).astype(BF16)
        for hh in range(2):
            vsl = slice(hh * LANE, (hh + 1) * LANE)
            acc_ref[hh] = al_ref[hh] * acc_ref[hh] + _bdot(p_ref[hh], v_ref[pl.ds(start, tq), vsl])

    def full_step(ki, carry):
        kv_step(ki, False)
        return carry

    lax.fori_loop(0, qi, full_step, 0)
    kv_step(qi, True)
    a0 = acc_ref[0]
    a1 = acc_ref[1]
    l0 = a0[:, MLA_VDIM:MLA_VDIM + 1]
    l1 = a1[:, 0:1]
    lane = lax.broadcasted_iota(jnp.int32, (tq, LANE), 1)
    o_ref[...] = jnp.where(lane < MLA_VDIM, a0 / l0, a1 / l1)


def _attention(q, k, v, batch, seq):
    T = q.shape[0]
    tq = 512
    nq = seq // tq
    npair = MLA_HEADS // 2
    return pl.pallas_call(
        functools.partial(_attn_kernel, tq=tq),
        grid=(batch, npair, nq),
        in_specs=[
            pl.BlockSpec((tq, 2 * HEAD_PAD), lambda b, h, i: (b * nq + i, h)),
            pl.BlockSpec((seq, 2 * HEAD_PAD), lambda b, h, i: (b, h)),
            pl.BlockSpec((seq, 2 * HEAD_PAD), lambda b, h, i: (b, h)),
        ],
        out_specs=pl.BlockSpec((tq, 2 * MLA_VDIM), lambda b, h, i: (b * nq + i, h)),
        out_shape=jax.ShapeDtypeStruct((T, MLA_HEADS * MLA_VDIM), F32),
        scratch_shapes=[
            pltpu.VMEM((2, tq, tq), F32),
            pltpu.VMEM((2, tq, tq), BF16),
            pltpu.VMEM((2, tq, LANE), F32),
            pltpu.VMEM((2, tq, LANE), F32),
            pltpu.VMEM((2, tq, LANE), F32),
        ],
        compiler_params=_cparams(("arbitrary", "arbitrary", "arbitrary")),
        name="attention",
    )(q, k, v)


def _gmlp_kernel(p_ref, g_ref, b_ref, w_ref, bs_ref, o_ref, *, tm):
    x = p_ref[...].astype(F32)
    ge = 0.5 * x * (1.0 + lax.erf(x * (1.0 / math.sqrt(2.0))))
    u = ge[:, :GM_WIDTH]
    v = ge[:, GM_WIDTH:]
    mu = jnp.mean(v, axis=-1, keepdims=True)
    vc = v - mu
    var = jnp.mean(vc * vc, axis=-1, keepdims=True)
    vn = (vc * lax.rsqrt(var + LN_EPS) * g_ref[...] + b_ref[...]).astype(BF16)
    row = lax.broadcasted_iota(jnp.int32, (GM_BLOCK, GM_BLOCK), 0)
    col = lax.broadcasted_iota(jnp.int32, (GM_BLOCK, GM_BLOCK), 1)
    mask = (col // CHUNK) <= (row // CHUNK)
    for g in range(GM_GROUPS):
        w = jnp.where(mask, w_ref[g], 0.0).astype(BF16)
        csl = slice(g * GM_GROUP_CH, (g + 1) * GM_GROUP_CH)
        for blk in range(tm // GM_BLOCK):
            rsl = slice(blk * GM_BLOCK, (blk + 1) * GM_BLOCK)
            s = _bdot(w, vn[rsl, csl]) + bs_ref[g]
            o_ref[rsl, csl] = u[rsl, csl] * s


def _gmlp(p, ln_g, ln_b, w_s, b_s):
    T = p.shape[0]
    tm = 512
    return pl.pallas_call(
        functools.partial(_gmlp_kernel, tm=tm),
        grid=(T // tm,),
        in_specs=[
            pl.BlockSpec((tm, 2 * GM_WIDTH), lambda i: (i, OFF_GM // (2 * GM_WIDTH))),
            pl.BlockSpec((1, GM_WIDTH), lambda i: (0, 0)),
            pl.BlockSpec((1, GM_WIDTH), lambda i: (0, 0)),
            pl.BlockSpec((GM_GROUPS, GM_BLOCK, GM_BLOCK), lambda i: (0, 0, 0)),
            pl.BlockSpec((GM_GROUPS, GM_BLOCK, 1), lambda i: (0, 0, 0)),
        ],
        out_specs=pl.BlockSpec((tm, GM_WIDTH), lambda i: (i, 0)),
        out_shape=jax.ShapeDtypeStruct((T, GM_WIDTH), F32),
        compiler_params=_cparams(("arbitrary",)),
        name="gmlp",
    )(p, ln_g, ln_b, w_s, b_s.reshape(GM_GROUPS, GM_BLOCK, 1))


def _rw_prep_kernel(*refs, tm, has_vmix):
    if has_vmix:
        (r_ref, k_ref, v_ref, lat_ref, vf_ref, mur_ref, muk_ref, muv_ref, mul_ref, w0_ref, w2_ref, a0_ref, a2_ref,
         kk_ref, ka_ref, v0_ref, v1_ref, v2_ref, ones_ref,
         ro_ref, lwo_ref, ko_ref, vo_ref, ao_ref, bo_ref, cr_ref, ck_ref, cv_ref, cl_ref) = refs
    else:
        (r_ref, k_ref, v_ref, lat_ref, mur_ref, muk_ref, muv_ref, mul_ref, w0_ref, w2_ref, a0_ref, a2_ref,
         kk_ref, ka_ref, ones_ref,
         ro_ref, lwo_ref, ko_ref, vo_ref, ao_ref, bo_ref, cr_ref, ck_ref, cv_ref, cl_ref) = refs

    @pl.when(pl.program_id(1) == 0)
    def _():
        for c in (cr_ref, ck_ref, cv_ref, cl_ref):
            c[...] = jnp.zeros_like(c)

    def shifted(x_ref, carry_ref, mu_ref):
        x = x_ref[...].astype(F32)
        rolled = pltpu.roll(x, 1, 0)
        rid = lax.broadcasted_iota(jnp.int32, x.shape, 0)
        prev = jnp.where(rid == 0, carry_ref[0:1, :], rolled)
        carry_ref[0:1, :] = x[tm - 1:tm, :]
        return x + (prev - x) * mu_ref[...]

    r = shifted(r_ref, cr_ref, mur_ref)
    k = shifted(k_ref, ck_ref, muk_ref)
    v = shifted(v_ref, cv_ref, muv_ref)
    lat = shifted(lat_ref, cl_ref, mul_ref)

    ww = w0_ref[...] + _bdot(jnp.tanh(lat), w2_ref[...])
    nw = -ww
    softplus = jnp.maximum(nw, 0.0) + jnp.log(1.0 + jnp.exp(-jnp.abs(nw)))
    w_log = -softplus - 0.5
    lwo_ref[...] = -jnp.exp(w_log)
    a = _sigmoid(a0_ref[...] + _bdot(lat, a2_ref[...]))
    if has_vmix:
        gate = _sigmoid(v0_ref[...] + _bdot(_bdot(v, v1_ref[...]), v2_ref[...]))
        v = v + (vf_ref[...] - v) * gate
    kk = k * kk_ref[...]
    n2 = _split3_dot(kk * kk, ones_ref[...])
    kk = kk / jnp.maximum(jnp.sqrt(n2), 1e-12)
    ro_ref[...] = r
    ko_ref[...] = k * (1.0 + (a - 1.0) * ka_ref[...])
    vo_ref[...] = v
    ao_ref[...] = -kk
    bo_ref[...] = kk * a


def _rw_prep(p, v_first, prm, batch, seq):
    T = p.shape[0]
    tm = 512
    nt = seq // tm
    has_vmix = v_first is not None
    W = RW_WIDTH
    rowp = lambda blk: (lambda b, j: (b * nt + j, blk))
    row0 = lambda b, j: (b * nt + j, 0)
    const = lambda b, j: (0, 0)
    vec = pl.BlockSpec((1, W), const)
    in_specs = [
        pl.BlockSpec((tm, W), rowp(OFF_R // W)),
        pl.BlockSpec((tm, W), rowp(OFF_K // W)),
        pl.BlockSpec((tm, W), rowp(OFF_V // W)),
        pl.BlockSpec((tm, LANE), rowp(OFF_LAT // LANE)),
    ]
    args = [p, p, p, p]
    if has_vmix:
        in_specs.append(pl.BlockSpec((tm, W), row0))
        args.append(v_first)
    in_specs += [vec, vec, vec, pl.BlockSpec((1, LANE), const), vec, pl.BlockSpec((LANE, W), const),
                 vec, pl.BlockSpec((LANE, W), const), vec, vec]
    args += [prm["mu_r"], prm["mu_k"], prm["mu_v"], prm["mu_l"], prm["w0"], prm["w2"], prm["a0"], prm["a2"],
             prm["k_k"], prm["k_a"]]
    if has_vmix:
        in_specs += [vec, pl.BlockSpec((W, LANE), const), pl.BlockSpec((LANE, W), const)]
        args += [prm["v0"], prm["v1"], prm["v2"]]
    in_specs.append(pl.BlockSpec((W, W), const))
    args.append(prm["head_ones"])
    out = pl.BlockSpec((tm, W), row0)
    return pl.pallas_call(
        functools.partial(_rw_prep_kernel, tm=tm, has_vmix=has_vmix),
        grid=(batch, nt),
        in_specs=in_specs,
        out_specs=[out] * 6,
        out_shape=[jax.ShapeDtypeStruct((T, W), F32)] * 6,
        scratch_shapes=[pltpu.VMEM((8, W), F32)] * 3 + [pltpu.VMEM((8, LANE), F32)],
        compiler_params=_cparams(("arbitrary", "arbitrary")),
        name="rw_prep",
    )(*args)


def _unit_lower_inverse(lows, row, col, n):
    eye = jnp.where(row == col, 1.0, 0.0)
    base = 16
    same = jnp.where((row // base) == (col // base), 1.0, 0.0)
    pws = [low * same for low in lows]
    invs = [eye + pw for pw in pws]
    span = 2
    while span < base:
        pws = [_bdot(pw, pw) for pw in pws]
        invs = [inv + _bdot(inv, pw) for inv, pw in zip(invs, pws)]
        span *= 2
    size = base
    while size < n:
        pair = (row // (2 * size)) == (col // (2 * size))
        sel = jnp.where(pair, jnp.where((row // size) == (col // size), 0.0, 1.0), 0.0)
        tmp = [_bdot(inv, low * sel) for inv, low in zip(invs, lows)]
        invs = [inv + _bdot(t, inv) for inv, t in zip(invs, tmp)]
        size *= 2
    return invs


def _wkv_kernel(r_ref, lw_ref, k_ref, v_ref, a_ref, b_ref, rk_ref, g_ref, beta_ref, ones_ref, o_ref, s_ref,
                *, C, nch):
    @pl.when(pl.program_id(1) == 0)
    def _():
        s_ref[...] = jnp.zeros_like(s_ref)

    P = 2 * C
    R = nch * C
    r = r_ref[...]
    lw = lw_ref[...]
    k = k_ref[...]
    v = v_ref[...]
    a_s = a_ref[...]
    b_s = b_ref[...]
    rowr = lax.broadcasted_iota(jnp.int32, (R, R), 0)
    colr = lax.broadcasted_iota(jnp.int32, (R, R), 1)
    tri = jnp.where((rowr // C) == (colr // C), jnp.where(rowr >= colr, 1.0, 0.0), 0.0).astype(BF16)
    cum = _split3_dot_rhs(tri, lw)
    e_in = jnp.exp(cum)
    e_inv = jnp.exp(-cum)
    a_t = a_s * jnp.exp(cum - lw)
    r_t = r * e_in
    b_t = b_s * e_inv
    k_t = k * e_inv
    rk = r * k * rk_ref[...]

    row = lax.broadcasted_iota(jnp.int32, (P, P), 0)
    col = lax.broadcasted_iota(jnp.int32, (P, P), 1)
    same_head = (row // C) == (col // C)
    strict = jnp.where(same_head, jnp.where((row % C) > (col % C), 1.0, 0.0), 0.0)
    incl = jnp.where(same_head, jnp.where((row % C) >= (col % C), 1.0, 0.0), 0.0)
    incl2 = jnp.concatenate([incl, incl], axis=1)
    even = lax.broadcasted_iota(jnp.int32, (C, LANE), 1) < RW_HEAD
    ones = ones_ref[...]

    def stack(x):
        return jnp.concatenate([jnp.where(even, x, 0.0), jnp.where(even, 0.0, x)], axis=0)

    npair = RW_HEADS // 2
    items = [(ci, pr) for ci in range(nch) for pr in range(npair)]
    rws = [slice(ci * C, (ci + 1) * C) for ci, _ in items]
    sls = [slice(pr * LANE, (pr + 1) * LANE) for _, pr in items]
    ars = [jnp.concatenate([stack(a_t[rw, sl]), stack(r_t[rw, sl])], axis=0) for rw, sl in zip(rws, sls)]
    bks = [jnp.concatenate([stack(b_t[rw, sl]), stack(k_t[rw, sl])], axis=0) for rw, sl in zip(rws, sls)]
    v_ps = [stack(v[rw, sl]) for rw, sl in zip(rws, sls)]
    gs = [_bdot(ar, bk, NT) for ar, bk in zip(ars, bks)]
    tinvs = _unit_lower_inverse([g[:P, :P] * strict for g in gs], row, col, C)
    akvs = [_bdot(g[:P, P:] * strict, vp) for g, vp in zip(gs, v_ps)]
    a_rs = [g[P:, :] * incl2 for g in gs]
    cum_last = [cum[(ci + 1) * C - 1:(ci + 1) * C, :] for ci in range(nch)]
    e_end = [jnp.exp(cum_last[ci] - cum[ci * C:(ci + 1) * C, :]) for ci in range(nch)]
    p_end = [jnp.exp(cl) for cl in cum_last]
    bkes = [jnp.concatenate([stack(b_s[rw, sl] * e_end[ci][:, sl]), stack(k[rw, sl] * e_end[ci][:, sl])], axis=0)
            for (ci, _), rw, sl in zip(items, rws, sls)]
    state = [s_ref[pr] for pr in range(npair)]
    ys = []
    for ci in range(nch):
        idx = [ci * npair + pr for pr in range(npair)]
        sas = [_bdot(ars[i], state[pr], NT) for pr, i in enumerate(idx)]
        us = [_bdot(tinvs[i], sa[:P] + akvs[i]) for sa, i in zip(sas, idx)]
        uvs = [jnp.concatenate([u, v_ps[i]], axis=0) for u, i in zip(us, idx)]
        state = [state[pr] * p_end[ci][:, sls[i]] + _bdot(uv, bkes[i], TN)
                 for pr, (uv, i) in enumerate(zip(uvs, idx))]
        y_pss = [sa[P:] + _bdot(a_rs[i], uv) for sa, uv, i in zip(sas, uvs, idx)]
        ys += [y_ps[:C] + y_ps[C:] for y_ps in y_pss]
    for pr in range(npair):
        s_ref[pr] = state[pr]
    mus = [_split3_dot(y, ones) * (1.0 / RW_HEAD) for y in ys]
    ycs = [y - mu for y, mu in zip(ys, mus)]
    vrs = [_split3_dot(yc * yc, ones) * (1.0 / RW_HEAD) for yc in ycs]
    bns = [_split3_dot(rk[rw, sl], ones) * v[rw, sl] for rw, sl in zip(rws, sls)]
    for rw, sl, yc, var, bonus in zip(rws, sls, ycs, vrs, bns):
        o_ref[rw, sl] = yc * lax.rsqrt(var + RW_LN_EPS) * g_ref[:, sl] + beta_ref[:, sl] + bonus


def _split3_dot_rhs(m_bf16, x):
    x1 = x.astype(BF16)
    r1 = x - x1.astype(F32)
    x2 = r1.astype(BF16)
    x3 = (r1 - x2.astype(F32)).astype(BF16)
    return _bdot(m_bf16, x1) + _bdot(m_bf16, x2) + _bdot(m_bf16, x3)


def _wkv(r, lw, k, v, a_s, b_s, r_k, lnx_g, lnx_b, pair_ones, batch, seq):
    T = r.shape[0]
    C = CHUNK
    nch = 4
    R = nch * C
    nc = seq // R
    W = RW_WIDTH
    row = pl.BlockSpec((R, W), lambda b, c: (b * nc + c, 0))
    vec = pl.BlockSpec((1, W), lambda b, c: (0, 0))
    return pl.pallas_call(
        functools.partial(_wkv_kernel, C=C, nch=nch),
        grid=(batch, nc),
        in_specs=[row] * 6 + [vec] * 3 + [pl.BlockSpec((LANE, LANE), lambda b, c: (0, 0))],
        out_specs=row,
        out_shape=jax.ShapeDtypeStruct((T, W), F32),
        scratch_shapes=[pltpu.VMEM((RW_HEADS // 2, LANE, LANE), F32)],
        compiler_params=_cparams(("arbitrary", "arbitrary")),
        name="wkv7",
    )(r, lw, k, v, a_s, b_s, r_k, lnx_g, lnx_b, pair_ones)


def _merge_kernel(ya_ref, yg_ref, yr_ref, za_ref, zg_ref, zr_ref, ga_ref, gg_ref, gr_ref,
                  x_ref, gate_ref, pg_ref, wbr_ref, wout_ref, o_ref):
    acc = None
    for n, (y_ref, z_ref, g_ref) in enumerate(((ya_ref, za_ref, ga_ref), (yg_ref, zg_ref, gg_ref),
                                                (yr_ref, zr_ref, gr_ref))):
        z = z_ref[...].astype(F32)
        br = (y_ref[...] * (z * _sigmoid(z))).astype(BF16)
        pr = _bdot(br, wbr_ref[n]) * _sigmoid(g_ref[...].astype(F32))
        acc = pr if acc is None else acc + pr
    y = _bdot(acc, wout_ref[...])
    yn = y * lax.rsqrt(jnp.mean(y * y, axis=-1, keepdims=True) + EPS) * pg_ref[...]
    o_ref[...] = x_ref[...] + gate_ref[0] * yn


def _merge(y_mla, y_gm, y_rw, p, x2d, gate, post_g, w_br, w_out, layer, seq):
    T, D = x2d.shape
    tm = 256
    per_b = seq // tm
    rowp = lambda blk: (lambda i: (i, blk))
    yspec = pl.BlockSpec((tm, BW), rowp(0))
    return pl.pallas_call(
        _merge_kernel,
        grid=(T // tm,),
        in_specs=[yspec, yspec, yspec]
        + [pl.BlockSpec((tm, BW), rowp(OFF_Z // BW + n)) for n in range(3)]
        + [pl.BlockSpec((tm, D), rowp(OFF_G // D + n)) for n in range(3)]
        + [
            pl.BlockSpec((tm, D), rowp(0)),
            pl.BlockSpec((1, 1, D), lambda i: (i // per_b, 0, 0)),
            pl.BlockSpec((1, D), lambda i: (0, 0)),
            pl.BlockSpec((None, 3, BW, D), lambda i: (layer, 0, 0, 0)),
            pl.BlockSpec((None, D, D), lambda i: (layer, 0, 0)),
        ],
        out_specs=pl.BlockSpec((tm, D), rowp(0)),
        out_shape=jax.ShapeDtypeStruct((T, D), F32),
        compiler_params=_cparams(("arbitrary",)),
        name="merge",
    )(y_mla, y_gm, y_rw, p, p, p, p, p, p, x2d, gate, post_g, w_br, w_out)


def _relayout_w_in(w_in):
    L, D, _ = w_in.shape
    w = w_in.astype(BF16)
    o_q, o_kv, o_kr, o_gm = 0, MLA_Q_RANK, MLA_Q_RANK + MLA_KV_RANK, MLA_Q_RANK + MLA_KV_RANK + MLA_ROPE
    o_rw = o_gm + 2 * GM_WIDTH
    o_lat = o_rw + 3 * RW_WIDTH
    o_z = o_lat + 2 * RW_LORA
    o_g = o_z + 3 * BW
    half = MLA_ROPE // 2
    kr = w[:, :, o_kr:o_kr + MLA_ROPE]
    krs = jnp.concatenate([kr[:, :, half:], kr[:, :, :half]], axis=-1)
    z64 = jnp.zeros((L, D, MLA_NOPE), BF16)
    z32 = jnp.zeros((L, D, LANE - MLA_NOPE - MLA_ROPE), BF16)
    pad = jnp.zeros((L, D, NP - OFF_LAT - LANE), BF16)
    cols = [
        w[:, :, o_gm:o_lat],
        w[:, :, o_z:o_g],
        w[:, :, o_g:],
        w[:, :, o_q:o_kr],
        z64, kr, z32,
        z64, krs, z32,
        w[:, :, o_lat:o_z],
        pad,
    ]
    return jnp.concatenate(cols, axis=-1)


def _relayout_mla(w_uq, w_ukv):
    L = w_uq.shape[0]
    dq = MLA_NOPE + MLA_ROPE
    half = MLA_ROPE // 2
    wq = w_uq.astype(BF16).reshape(L, MLA_Q_RANK, MLA_HEADS, dq)
    padq = ((0, 0), (0, 0), (0, 0), (0, HEAD_PAD - dq))
    wq_p = jnp.pad(wq, padq).reshape(L, MLA_Q_RANK, MLA_HEADS * HEAD_PAD)
    rope = wq[..., MLA_NOPE:]
    rot = jnp.concatenate([jnp.zeros_like(wq[..., :MLA_NOPE]), rope[..., half:], rope[..., :half]], axis=-1)
    wqr_p = jnp.pad(rot, padq).reshape(L, MLA_Q_RANK, MLA_HEADS * HEAD_PAD)
    wkv = w_ukv.astype(BF16).reshape(L, MLA_KV_RANK, MLA_HEADS, MLA_NOPE + MLA_VDIM)
    wk_p = jnp.pad(wkv[..., :MLA_NOPE], ((0, 0), (0, 0), (0, 0), (0, HEAD_PAD - MLA_NOPE)))
    wk_p = wk_p.reshape(L, MLA_KV_RANK, MLA_HEADS * HEAD_PAD)
    wv = wkv[..., MLA_NOPE:].reshape(L, MLA_KV_RANK, MLA_HEADS // 2, 2, MLA_VDIM)
    zv = jnp.zeros_like(wv[:, :, :, 0])
    wv_p = jnp.concatenate([wv[:, :, :, 0], zv, zv, wv[:, :, :, 1]], axis=-1)
    wv_p = wv_p.reshape(L, MLA_KV_RANK, MLA_HEADS * HEAD_PAD)
    return wq_p, wqr_p, wk_p, wv_p


def kernel(x, c, positions, pre_g, post_g, w_ada, b_ada, w_in, mla_q_norm, mla_w_uq, mla_kv_norm, mla_w_ukv,
           gm_ln_g, gm_ln_b, gm_w_s, gm_b_s, rw_mu, rw_w0, rw_w2, rw_a0, rw_a2, rw_k_k, rw_k_a, rw_r_k,
           rw_lnx_g, rw_lnx_b, rw_v0, rw_v1, rw_v2, w_br, w_out):
    B, S, D = x.shape
    L = w_in.shape[0]
    T = B * S
    x2d = x.reshape(T, D)

    c_pad = jnp.pad(c, ((0, 8 - B), (0, 0)))
    mods = _ada(c_pad, w_ada, b_ada)[:, :B]
    ctab, stab = _rope_tables(positions)

    w_in_p = _relayout_w_in(w_in)
    wq_p, wqr_p, wk_p, wv_p = _relayout_mla(mla_w_uq, mla_w_ukv)
    w_br_b = w_br.astype(BF16)
    w_out_b = w_out.astype(BF16)
    zl = jnp.zeros((RW_LORA, RW_WIDTH), F32)
    hid = np.arange(RW_WIDTH) // RW_HEAD
    head_ones = jnp.asarray((hid[:, None] == hid[None, :]).astype(np.float32)).astype(BF16)
    pair_lane = np.arange(MLA_HEADS * HEAD_PAD) % (2 * HEAD_PAD)
    v_ones = jnp.asarray(((pair_lane >= MLA_VDIM) & (pair_lane < 2 * HEAD_PAD - MLA_VDIM)).astype(np.float32)[None, :])

    v_first = None
    for l in range(L):
        shift = mods[l, :, :D].reshape(B, 1, D)
        scale = mods[l, :, D:2 * D].reshape(B, 1, D)
        gate = mods[l, :, 2 * D:].reshape(B, 1, D)
        p = _inproj(x2d, scale, shift, pre_g[l].reshape(1, D), w_in_p, l, S)

        q, k, v = _mla_prep(p, ctab, stab, mla_q_norm[l].reshape(1, -1), mla_kv_norm[l].reshape(1, -1),
                            wq_p, wqr_p, wk_p, wv_p, v_ones, l)
        y_mla = _attention(q, k, v, B, S)

        y_gm = _gmlp(p, gm_ln_g[l].reshape(1, -1), gm_ln_b[l].reshape(1, -1), gm_w_s[l], gm_b_s[l])

        mu = rw_mu[l]
        prm = {
            "mu_r": mu[:RW_WIDTH].reshape(1, -1),
            "mu_k": mu[RW_WIDTH:2 * RW_WIDTH].reshape(1, -1),
            "mu_v": mu[2 * RW_WIDTH:3 * RW_WIDTH].reshape(1, -1),
            "mu_l": mu[3 * RW_WIDTH:].reshape(1, -1),
            "w0": rw_w0[l].reshape(1, -1),
            "w2": jnp.concatenate([rw_w2[l], zl], axis=0),
            "a0": rw_a0[l].reshape(1, -1),
            "a2": jnp.concatenate([zl, rw_a2[l]], axis=0),
            "k_k": rw_k_k[l].reshape(1, -1),
            "k_a": rw_k_a[l].reshape(1, -1),
            "head_ones": head_ones,
        }
        if l > 0:
            prm["v0"] = rw_v0[l - 1].reshape(1, -1)
            prm["v1"] = jnp.pad(rw_v1[l - 1], ((0, 0), (0, LANE - RW_V_LORA)))
            prm["v2"] = jnp.pad(rw_v2[l - 1], ((0, LANE - RW_V_LORA), (0, 0)))
        r_s, lw_s, k_s, v_s, a_s, b_s = _rw_prep(p, v_first, prm, B, S)
        if l == 0:
            v_first = v_s
        y_rw = _wkv(r_s, lw_s, k_s, v_s, a_s, b_s, rw_r_k[l].reshape(1, -1), rw_lnx_g[l].reshape(1, -1),
                    rw_lnx_b[l].reshape(1, -1), head_ones[:LANE, :LANE], B, S)

        x2d = _merge(y_mla, y_gm, y_rw, p, x2d, gate, post_g[l].reshape(1, D), w_br_b, w_out_b, l, S)
    return x2d.reshape(B, S, D)
```

```python
import functools
import math

import jax
import jax.numpy as jnp
import numpy as np
from jax import lax
from jax.experimental import pallas as pl
from jax.experimental.pallas import tpu as pltpu

F32 = jnp.float32
BF16 = jnp.bfloat16
HIGHEST = lax.Precision.HIGHEST

CHUNK = 64
EPS = 1e-6
LN_EPS = 1e-5
MLA_HEADS = 8
MLA_Q_RANK = 256
MLA_KV_RANK = 128
MLA_NOPE = 64
MLA_ROPE = 32
MLA_VDIM = 64
ROPE_THETA = 10000.0
GM_GROUPS = 4
GM_GROUP_CH = 128
GM_WIDTH = 512
GM_BLOCK = 128
RW_HEADS = 8
RW_HEAD = 64
RW_WIDTH = 512
RW_LORA = 64
RW_V_LORA = 32
RW_LN_EPS = 64e-5
BW = 512
LANE = 128
HEAD_PAD = 128

OFF_GM = 0
OFF_R, OFF_K, OFF_V = 1024, 1536, 2048
OFF_Z = 2560
OFF_G = 4096
OFF_QLAT = 7168
OFF_KVLAT = 7424
OFF_KR = 7552
OFF_KRS = 7680
OFF_LAT = 7808
NP = 8192

VMEM_LIMIT = 56 * 1024 * 1024
NT = (((1,), (1,)), ((), ()))
TN = (((0,), (0,)), ((), ()))


def _cparams(sem):
    return pltpu.CompilerParams(dimension_semantics=sem, vmem_limit_bytes=VMEM_LIMIT)


def _mm(a, b):
    return jnp.dot(a, b, preferred_element_type=F32, precision=HIGHEST)


def _bdot(a, b, dims=(((1,), (0,)), ((), ()))):
    return lax.dot_general(a.astype(BF16), b.astype(BF16), dims, preferred_element_type=F32)


def _sigmoid(x):
    return 1.0 / (1.0 + jnp.exp(-x))


def _ada_kernel(c_ref, w_ref, b_ref, o_ref):
    c = c_ref[...]
    ca = c * _sigmoid(c)
    o_ref[0] = _mm(ca, w_ref[0]) + b_ref[0]


def _ada(c_pad, w_ada, b_ada):
    L, D, D3 = w_ada.shape
    tn = 1024
    return pl.pallas_call(
        _ada_kernel,
        grid=(L, D3 // tn),
        in_specs=[
            pl.BlockSpec((8, D), lambda l, j: (0, 0)),
            pl.BlockSpec((1, D, tn), lambda l, j: (l, 0, j)),
            pl.BlockSpec((1, 1, tn), lambda l, j: (l, 0, j)),
        ],
        out_specs=pl.BlockSpec((1, 8, tn), lambda l, j: (l, 0, j)),
        out_shape=jax.ShapeDtypeStruct((L, 8, D3), F32),
        compiler_params=_cparams(("arbitrary", "arbitrary")),
        name="ada",
    )(c_pad, w_ada, b_ada.reshape(L, 1, D3))


def _rope_kernel(pos_ref, freq_ref, sign_ref, c_ref, s_ref):
    pos = pos_ref[...].astype(F32)
    ang = pos * freq_ref[...]
    lane = lax.broadcasted_iota(jnp.int32, ang.shape, 1)
    is_rope = (lane >= MLA_NOPE) & (lane < MLA_NOPE + MLA_ROPE)
    c_ref[...] = jnp.where(is_rope, jnp.cos(ang), jnp.where(lane < MLA_NOPE, 1.0, 0.0))
    s_ref[...] = jnp.sin(ang) * sign_ref[...]


def _rope_tables(positions):
    T = positions.size
    tm = 2048
    inv_freq = ROPE_THETA ** (-np.arange(0, MLA_ROPE, 2, dtype=np.float32) / MLA_ROPE)
    half = MLA_ROPE // 2
    freq = np.zeros((1, LANE), np.float32)
    freq[0, MLA_NOPE:MLA_NOPE + half] = inv_freq
    freq[0, MLA_NOPE + half:MLA_NOPE + MLA_ROPE] = inv_freq
    sign = np.zeros((1, LANE), np.float32)
    sign[0, MLA_NOPE:MLA_NOPE + half] = -1.0
    sign[0, MLA_NOPE + half:MLA_NOPE + MLA_ROPE] = 1.0
    return pl.pallas_call(
        _rope_kernel,
        grid=(T // tm,),
        in_specs=[
            pl.BlockSpec((tm, 1), lambda i: (i, 0)),
            pl.BlockSpec((1, LANE), lambda i: (0, 0)),
            pl.BlockSpec((1, LANE), lambda i: (0, 0)),
        ],
        out_specs=[pl.BlockSpec((tm, LANE), lambda i: (i, 0))] * 2,
        out_shape=[jax.ShapeDtypeStruct((T, LANE), F32)] * 2,
        compiler_params=_cparams(("arbitrary",)),
        name="rope_tables",
    )(positions.reshape(T, 1), jnp.asarray(freq), jnp.asarray(sign))


def _inproj_kernel(x_ref, sc_ref, sh_ref, g_ref, w_ref, o_ref, h_ref):
    @pl.when(pl.program_id(1) == 0)
    def _():
        x = x_ref[...]
        ms = jnp.mean(x * x, axis=-1, keepdims=True)
        y = x * lax.rsqrt(ms + EPS) * g_ref[...]
        h_ref[...] = (y * (1.0 + sc_ref[0]) + sh_ref[0]).astype(BF16)

    o_ref[...] = _bdot(h_ref[...], w_ref[...]).astype(BF16)


def _inproj(x2d, scale, shift, pre_g, w_in_p, layer, seq):
    T, D = x2d.shape
    tm, tn = 1024, 1024
    per_b = seq // tm
    return pl.pallas_call(
        _inproj_kernel,
        grid=(T // tm, NP // tn),
        in_specs=[
            pl.BlockSpec((tm, D), lambda i, j: (i, 0)),
            pl.BlockSpec((1, 1, D), lambda i, j: (i // per_b, 0, 0)),
            pl.BlockSpec((1, 1, D), lambda i, j: (i // per_b, 0, 0)),
            pl.BlockSpec((1, D), lambda i, j: (0, 0)),
            pl.BlockSpec((None, D, tn), lambda i, j: (layer, 0, j)),
        ],
        out_specs=pl.BlockSpec((tm, tn), lambda i, j: (i, j)),
        out_shape=jax.ShapeDtypeStruct((T, NP), BF16),
        scratch_shapes=[pltpu.VMEM((tm, D), BF16)],
        compiler_params=_cparams(("arbitrary", "arbitrary")),
        name="inproj",
    )(x2d, scale, shift, pre_g, w_in_p)


def _mla_prep_kernel(ql_ref, kvl_ref, kr_ref, krs_ref, ct_ref, st_ref, qg_ref, kvg_ref,
                     wq_ref, wqr_ref, wk_ref, wv_ref, vone_ref, q_ref, k_ref, v_ref, *, scale):
    ql = ql_ref[...].astype(F32)
    qn = (ql * lax.rsqrt(jnp.mean(ql * ql, axis=-1, keepdims=True) + EPS) * qg_ref[...]).astype(BF16)
    kvl = kvl_ref[...].astype(F32)
    kvn = (kvl * lax.rsqrt(jnp.mean(kvl * kvl, axis=-1, keepdims=True) + EPS) * kvg_ref[...]).astype(BF16)
    ct = ct_ref[...]
    st = st_ref[...]
    kpe = kr_ref[...].astype(F32) * ct + krs_ref[...].astype(F32) * st
    v_ref[...] = (_bdot(kvn, wv_ref[...]) + vone_ref[...]).astype(BF16)
    for h in range(MLA_HEADS):
        sl = slice(h * HEAD_PAD, (h + 1) * HEAD_PAD)
        q = _bdot(qn, wq_ref[:, sl]) * ct + _bdot(qn, wqr_ref[:, sl]) * st
        q_ref[:, sl] = (q * scale).astype(BF16)
        k_ref[:, sl] = (_bdot(kvn, wk_ref[:, sl]) + kpe).astype(BF16)


def _mla_prep(p, ctab, stab, qg, kvg, wq, wqr, wk, wv, vone, layer):
    T = p.shape[0]
    tm = 512
    HP = MLA_HEADS * HEAD_PAD
    scale = float((MLA_NOPE + MLA_ROPE) ** -0.5) * math.log2(math.e)
    row = lambda blk: (lambda i: (i, blk))
    const = lambda i: (0, 0)
    wspec = lambda rows: pl.BlockSpec((None, rows, HP), lambda i: (layer, 0, 0))
    return pl.pallas_call(
        functools.partial(_mla_prep_kernel, scale=scale),
        grid=(T // tm,),
        in_specs=[
            pl.BlockSpec((tm, MLA_Q_RANK), row(OFF_QLAT // MLA_Q_RANK)),
            pl.BlockSpec((tm, LANE), row(OFF_KVLAT // LANE)),
            pl.BlockSpec((tm, LANE), row(OFF_KR // LANE)),
            pl.BlockSpec((tm, LANE), row(OFF_KRS // LANE)),
            pl.BlockSpec((tm, LANE), row(0)),
            pl.BlockSpec((tm, LANE), row(0)),
            pl.BlockSpec((1, MLA_Q_RANK), const),
            pl.BlockSpec((1, MLA_KV_RANK), const),
            wspec(MLA_Q_RANK),
            wspec(MLA_Q_RANK),
            wspec(MLA_KV_RANK),
            wspec(MLA_KV_RANK),
            pl.BlockSpec((1, HP), const),
        ],
        out_specs=[pl.BlockSpec((tm, HP), row(0))] * 3,
        out_shape=[jax.ShapeDtypeStruct((T, HP), BF16)] * 3,
        compiler_params=_cparams(("arbitrary",)),
        name="mla_prep",
    )(p, p, p, p, ctab, stab, qg, kvg, wq, wqr, wk, wv, vone)


def _attn_kernel(q_ref, k_ref, v_ref, o_ref, s_ref, p_ref, m_ref, al_ref, acc_ref, *, tq):
    qi = pl.program_id(2)
    m_ref[...] = jnp.full(m_ref.shape, -1e30, F32)
    acc_ref[...] = jnp.zeros(acc_ref.shape, F32)
    col = lax.broadcasted_iota(jnp.int32, (CHUNK, tq), 1)
    strips = [slice(r * CHUNK, (r + 1) * CHUNK) for r in range(tq // CHUNK)]

    def kv_step(ki, masked):
        start = pl.multiple_of(ki * tq, tq)
        for hh in range(2):
            hsl = slice(hh * HEAD_PAD, (hh + 1) * HEAD_PAD)
            s_ref[hh] = _bdot(q_ref[:, hsl], k_ref[pl.ds(start, tq), hsl], NT)
        if masked:
            for hh in range(2):
                for r, rows in enumerate(strips):
                    s_ref[hh, rows, :] = jnp.where(col < (r + 1) * CHUNK, s_ref[hh, rows, :], -1e30)
        for hh in range(2):
            m_old = m_ref[hh]
            m_new = jnp.maximum(m_old, jnp.max(s_ref[hh], axis=-1, keepdims=True))
            al_ref[hh] = jnp.exp2(m_old - m_new)
            m_ref[hh] = m_new
        for hh in range(2):
            m_new = m_ref[hh]
            for j in range(tq // LANE):
                csl = slice(j * LANE, (j + 1) * LANE)
                p_ref[hh, :, csl] = jnp.exp2(s_ref[hh, :, csl] - m_new).astype(BF16)
        for hh in range(2):
            vsl = slice(hh * LANE, (hh + 1) * LANE)
            acc_ref[hh] = al_ref[hh] * acc_ref[hh] + _bdot(p_ref[hh], v_ref[pl.ds(start, tq), vsl])

    def full_step(ki, carry):
        kv_step(ki, False)
        return carry

    lax.fori_loop(0, qi, full_step, 0)
    kv_step(qi, True)
    a0 = acc_ref[0]
    a1 = acc_ref[1]
    l0 = a0[:, MLA_VDIM:MLA_VDIM + 1]
    l1 = a1[:, 0:1]
    lane = lax.broadcasted_iota(jnp.int32, (tq, LANE), 1)
    o_ref[...] = jnp.where(lane < MLA_VDIM, a0 / l0, a1 / l1)


def _attention(q, k, v, batch, seq):
    T = q.shape[0]
    tq = 512
    nq = seq // tq
    npair = MLA_HEADS // 2
    return pl.pallas_call(
        functools.partial(_attn_kernel, tq=tq),
        grid=(batch, npair, nq),
        in_specs=[
            pl.BlockSpec((tq, 2 * HEAD_PAD), lambda b, h, i: (b * nq + i, h)),
            pl.BlockSpec((seq, 2 * HEAD_PAD), lambda b, h, i: (b, h)),
            pl.BlockSpec((seq, 2 * HEAD_PAD), lambda b, h, i: (b, h)),
        ],
        out_specs=pl.BlockSpec((tq, 2 * MLA_VDIM), lambda b, h, i: (b * nq + i, h)),
        out_shape=jax.ShapeDtypeStruct((T, MLA_HEADS * MLA_VDIM), F32),
        scratch_shapes=[
            pltpu.VMEM((2, tq, tq), F32),
            pltpu.VMEM((2, tq, tq), BF16),
            pltpu.VMEM((2, tq, LANE), F32),
            pltpu.VMEM((2, tq, LANE), F32),
            pltpu.VMEM((2, tq, LANE), F32),
        ],
        compiler_params=_cparams(("arbitrary", "arbitrary", "arbitrary")),
        name="attention",
    )(q, k, v)


def _gmlp_kernel(p_ref, g_ref, b_ref, w_ref, bs_ref, o_ref, *, tm):
    x = p_ref[...].astype(F32)
    ge = 0.5 * x * (1.0 + lax.erf(x * (1.0 / math.sqrt(2.0))))
    u = ge[:, :GM_WIDTH]
    v = ge[:, GM_WIDTH:]
    mu = jnp.mean(v, axis=-1, keepdims=True)
    vc = v - mu
    var = jnp.mean(vc * vc, axis=-1, keepdims=True)
    vn = (vc * lax.rsqrt(var + LN_EPS) * g_ref[...] + b_ref[...]).astype(BF16)
    row = lax.broadcasted_iota(jnp.int32, (GM_BLOCK, GM_BLOCK), 0)
    col = lax.broadcasted_iota(jnp.int32, (GM_BLOCK, GM_BLOCK), 1)
    mask = (col // CHUNK) <= (row // CHUNK)
    for g in range(GM_GROUPS):
        w = jnp.where(mask, w_ref[g], 0.0).astype(BF16)
        csl = slice(g * GM_GROUP_CH, (g + 1) * GM_GROUP_CH)
        for blk in range(tm // GM_BLOCK):
            rsl = slice(blk * GM_BLOCK, (blk + 1) * GM_BLOCK)
            s = _bdot(w, vn[rsl, csl]) + bs_ref[g]
            o_ref[rsl, csl] = u[rsl, csl] * s


def _gmlp(p, ln_g, ln_b, w_s, b_s):
    T = p.shape[0]
    tm = 512
    return pl.pallas_call(
        functools.partial(_gmlp_kernel, tm=tm),
        grid=(T // tm,),
        in_specs=[
            pl.BlockSpec((tm, 2 * GM_WIDTH), lambda i: (i, OFF_GM // (2 * GM_WIDTH))),
            pl.BlockSpec((1, GM_WIDTH), lambda i: (0, 0)),
            pl.BlockSpec((1, GM_WIDTH), lambda i: (0, 0)),
            pl.BlockSpec((GM_GROUPS, GM_BLOCK, GM_BLOCK), lambda i: (0, 0, 0)),
            pl.BlockSpec((GM_GROUPS, GM_BLOCK, 1), lambda i: (0, 0, 0)),
        ],
        out_specs=pl.BlockSpec((tm, GM_WIDTH), lambda i: (i, 0)),
        out_shape=jax.ShapeDtypeStruct((T, GM_WIDTH), F32),
        compiler_params=_cparams(("arbitrary",)),
        name="gmlp",
    )(p, ln_g, ln_b, w_s, b_s.reshape(GM_GROUPS, GM_BLOCK, 1))


def _split3_dot_rhs(m_bf16, x):
    x1 = x.astype(BF16)
    r1 = x - x1.astype(F32)
    x2 = r1.astype(BF16)
    x3 = (r1 - x2.astype(F32)).astype(BF16)
    return _bdot(m_bf16, x1) + _bdot(m_bf16, x2) + _bdot(m_bf16, x3)


def _head_sums(x, even):
    s_even = jnp.sum(jnp.where(even, x, 0.0), axis=-1, keepdims=True)
    s_odd = jnp.sum(jnp.where(even, 0.0, x), axis=-1, keepdims=True)
    return jnp.where(even, s_even, s_odd)


def _bd2(x0, x1):
    z = jnp.zeros_like(x0)
    return jnp.concatenate([jnp.concatenate([x0, z], axis=1), jnp.concatenate([z, x1], axis=1)], axis=0)


def _bd(x):
    xb = x.astype(BF16)
    return _bd2(xb[:, :LANE], xb[:, LANE:])


def _unit_lower_inverse(lows, row, colm, n):
    eye = jnp.where(row == colm, 1.0, 0.0)
    base = 16
    same = jnp.where((row // base) == (colm // base), 1.0, 0.0)
    pws = [low * same for low in lows]
    invs = [eye + pw for pw in pws]
    span = 2
    while span < base:
        pws = [_bdot(pw, _bd(pw)) for pw in pws]
        invs = [inv + _bdot(inv, _bd(pw)) for inv, pw in zip(invs, pws)]
        span *= 2
    size = base
    while size < n:
        pair = (row // (2 * size)) == (colm // (2 * size))
        sel = jnp.where(pair, jnp.where((row // size) == (colm // size), 0.0, 1.0), 0.0)
        tmp = [_bdot(inv, _bd(low * sel)) for inv, low in zip(invs, lows)]
        invs = [inv + _bdot(t, _bd(inv)) for inv, t in zip(invs, tmp)]
        size *= 2
    return invs


def _rwkv_kernel(*refs, C, nch, has_vmix):
    if has_vmix:
        (r_ref, k_ref, v_ref, lat_ref, vf_ref, mur_ref, muk_ref, muv_ref, mul_ref, w0_ref, w2_ref, a0_ref, a2_ref,
         kk_ref, ka_ref, v0_ref, v1_ref, v2_ref, rk_ref, g_ref, beta_ref,
         o_ref, cr_ref, ck_ref, cv_ref, cl_ref, s_ref) = refs
    else:
        (r_ref, k_ref, v_ref, lat_ref, mur_ref, muk_ref, muv_ref, mul_ref, w0_ref, w2_ref, a0_ref, a2_ref,
         kk_ref, ka_ref, rk_ref, g_ref, beta_ref,
         o_ref, vo_ref, cr_ref, ck_ref, cv_ref, cl_ref, s_ref) = refs
    P = 2 * C
    R = nch * C

    @pl.when(pl.program_id(1) == 0)
    def _():
        s_ref[...] = jnp.zeros_like(s_ref)
        for c in (cr_ref, ck_ref, cv_ref, cl_ref):
            c[...] = jnp.zeros_like(c)

    def shifted(x_ref, carry_ref, mu_ref):
        x = x_ref[...].astype(F32)
        rolled = pltpu.roll(x, 1, 0)
        rid = lax.broadcasted_iota(jnp.int32, x.shape, 0)
        prev = jnp.where(rid == 0, carry_ref[0:1, :], rolled)
        carry_ref[0:1, :] = x[R - 1:R, :]
        return x + (prev - x) * mu_ref[...]

    r = shifted(r_ref, cr_ref, mur_ref)
    k = shifted(k_ref, ck_ref, muk_ref)
    v = shifted(v_ref, cv_ref, muv_ref)
    lat = shifted(lat_ref, cl_ref, mul_ref)
    ww = w0_ref[...] + _bdot(jnp.tanh(lat), w2_ref[...])
    nw = -ww
    softplus = jnp.maximum(nw, 0.0) + jnp.log(1.0 + jnp.exp(-jnp.abs(nw)))
    lw = -jnp.exp(-softplus - 0.5)
    a = _sigmoid(a0_ref[...] + _bdot(lat, a2_ref[...]))
    if has_vmix:
        gate = _sigmoid(v0_ref[...] + _bdot(_bdot(v, v1_ref[...]), v2_ref[...]))
        v = v + (vf_ref[...] - v) * gate
    else:
        vo_ref[...] = v
    even_r = lax.broadcasted_iota(jnp.int32, (R, LANE), 1) < RW_HEAD
    kk = k * kk_ref[...]
    n2 = jnp.concatenate([_head_sums((kk * kk)[:, j * LANE:(j + 1) * LANE], even_r)
                          for j in range(RW_WIDTH // LANE)], axis=1)
    kk = kk / jnp.maximum(jnp.sqrt(n2), 1e-12)
    k = k * (1.0 + (a - 1.0) * ka_ref[...])
    a_s = -kk
    b_s = kk * a

    rowr = lax.broadcasted_iota(jnp.int32, (R, R), 0)
    colr = lax.broadcasted_iota(jnp.int32, (R, R), 1)
    tri = jnp.where((rowr // C) == (colr // C), jnp.where(rowr >= colr, 1.0, 0.0), 0.0).astype(BF16)
    cum = _split3_dot_rhs(tri, lw)
    e_in = jnp.exp(cum)
    e_inv = jnp.exp(-cum)
    a_t = a_s * jnp.exp(cum - lw)
    r_t = r * e_in
    b_t = b_s * e_inv
    k_t = k * e_inv
    rk = r * k * rk_ref[...]

    row = lax.broadcasted_iota(jnp.int32, (P, 2 * P), 0)
    colm = lax.broadcasted_iota(jnp.int32, (P, 2 * P), 1) % P
    same_head = (row // C) == (colm // C)
    strict = jnp.where(same_head, jnp.where((row % C) > (colm % C), 1.0, 0.0), 0.0)
    incl = jnp.where(same_head, jnp.where((row % C) >= (colm % C), 1.0, 0.0), 0.0)
    even = lax.broadcasted_iota(jnp.int32, (C, LANE), 1) < RW_HEAD

    def stack(x):
        return jnp.concatenate([jnp.where(even, x, 0.0), jnp.where(even, 0.0, x)], axis=0)

    npair = RW_HEADS // 2
    items = [(ci, pr) for ci in range(nch) for pr in range(npair)]
    rws = [slice(ci * C, (ci + 1) * C) for ci, _ in items]
    sls = [slice(pr * LANE, (pr + 1) * LANE) for _, pr in items]
    duals = [(i, i + 1) for i in range(0, len(items), 2)]
    ars = [jnp.concatenate([stack(a_t[rw, sl]), stack(r_t[rw, sl])], axis=0).astype(BF16)
           for rw, sl in zip(rws, sls)]
    bks = [jnp.concatenate([stack(b_t[rw, sl]), stack(k_t[rw, sl])], axis=0) for rw, sl in zip(rws, sls)]
    v_ps = [stack(v[rw, sl]).astype(BF16) for rw, sl in zip(rws, sls)]
    gs = [_bdot(ar, bk, NT) for ar, bk in zip(ars, bks)]
    lows = [jnp.concatenate([gs[i][:P, :P], gs[j][:P, :P]], axis=1) * strict for i, j in duals]
    tinvs = _unit_lower_inverse(lows, row, colm, C)
    akvs = [_bdot(jnp.concatenate([gs[i][:P, P:], gs[j][:P, P:]], axis=1) * strict, _bd2(v_ps[i], v_ps[j]))
            for i, j in duals]
    a_rs = [g[P:, :] * incl for g in gs]
    cum_last = [cum[(ci + 1) * C - 1:(ci + 1) * C, :] for ci in range(nch)]
    e_end = [jnp.exp(cum_last[ci] - cum[ci * C:(ci + 1) * C, :]) for ci in range(nch)]
    p_end = [jnp.exp(cl) for cl in cum_last]
    bkes = [jnp.concatenate([stack(b_s[rw, sl] * e_end[ci][:, sl]), stack(k[rw, sl] * e_end[ci][:, sl])], axis=0)
            for (ci, _), rw, sl in zip(items, rws, sls)]
    state = [s_ref[pr] for pr in range(npair)]
    ys = []
    for ci in range(nch):
        base = ci * npair
        cdu = [(base + 2 * d, base + 2 * d + 1) for d in range(npair // 2)]
        sas = [_bdot(jnp.concatenate([ars[i], ars[j]], axis=1),
                     _bd2(state[i - base].astype(BF16), state[j - base].astype(BF16)), NT)
               for i, j in cdu]
        us = [_bdot(tinvs[i // 2], _bd(sa[:P] + akvs[i // 2])) for sa, (i, j) in zip(sas, cdu)]
        new_state = []
        for (i, j), sa, u in zip(cdu, sas, us):
            for half, it in enumerate((i, j)):
                hs = slice(half * LANE, (half + 1) * LANE)
                uv = jnp.concatenate([u[:, hs].astype(BF16), v_ps[it]], axis=0)
                new_state.append(state[it - base] * p_end[ci][:, sls[it]] + _bdot(uv, bkes[it], TN))
                y_ps = sa[P:, hs] + _bdot(a_rs[it], uv)
                ys.append(y_ps[:C] + y_ps[C:])
        state = new_state
    for pr in range(npair):
        s_ref[pr] = state[pr]
    mus = [_head_sums(y, even) * (1.0 / RW_HEAD) for y in ys]
    ycs = [y - mu for y, mu in zip(ys, mus)]
    vrs = [_head_sums(yc * yc, even) * (1.0 / RW_HEAD) for yc in ycs]
    bns = [_head_sums(rk[rw, sl], even) * v[rw, sl] for rw, sl in zip(rws, sls)]
    for rw, sl, yc, var, bonus in zip(rws, sls, ycs, vrs, bns):
        o_ref[rw, sl] = yc * lax.rsqrt(var + RW_LN_EPS) * g_ref[:, sl] + beta_ref[:, sl] + bonus


def _rwkv(p, v_first, prm, batch, seq):
    T = p.shape[0]
    C = CHUNK
    nch = 4
    R = nch * C
    nt = seq // R
    has_vmix = v_first is not None
    W = RW_WIDTH
    rowp = lambda blk: (lambda b, j: (b * nt + j, blk))
    row0 = lambda b, j: (b * nt + j, 0)
    const = lambda b, j: (0, 0)
    vec = pl.BlockSpec((1, W), const)
    in_specs = [
        pl.BlockSpec((R, W), rowp(OFF_R // W)),
        pl.BlockSpec((R, W), rowp(OFF_K // W)),
        pl.BlockSpec((R, W), rowp(OFF_V // W)),
        pl.BlockSpec((R, LANE), rowp(OFF_LAT // LANE)),
    ]
    args = [p, p, p, p]
    if has_vmix:
        in_specs.append(pl.BlockSpec((R, W), row0))
        args.append(v_first)
    in_specs += [vec, vec, vec, pl.BlockSpec((1, LANE), const), vec, pl.BlockSpec((LANE, W), const),
                 vec, pl.BlockSpec((LANE, W), const), vec, vec]
    args += [prm["mu_r"], prm["mu_k"], prm["mu_v"], prm["mu_l"], prm["w0"], prm["w2"], prm["a0"], prm["a2"],
             prm["k_k"], prm["k_a"]]
    if has_vmix:
        in_specs += [vec, pl.BlockSpec((W, LANE), const), pl.BlockSpec((LANE, W), const)]
        args += [prm["v0"], prm["v1"], prm["v2"]]
    in_specs += [vec, vec, vec]
    args += [prm["r_k"], prm["lnx_g"], prm["lnx_b"]]
    out = pl.BlockSpec((R, W), row0)
    n_out = 1 if has_vmix else 2
    res = pl.pallas_call(
        functools.partial(_rwkv_kernel, C=C, nch=nch, has_vmix=has_vmix),
        grid=(batch, nt),
        in_specs=in_specs,
        out_specs=[out] * n_out,
        out_shape=[jax.ShapeDtypeStruct((T, W), F32)] * n_out,
        scratch_shapes=[pltpu.VMEM((8, W), F32)] * 3 + [pltpu.VMEM((8, LANE), F32),
                                                        pltpu.VMEM((RW_HEADS // 2, LANE, LANE), F32)],
        compiler_params=_cparams(("arbitrary", "arbitrary")),
        name="rwkv7",
    )(*args)
    return (res[0], v_first) if has_vmix else (res[0], res[1])


def _merge_kernel(ya_ref, yg_ref, yr_ref, za_ref, zg_ref, zr_ref, ga_ref, gg_ref, gr_ref,
                  x_ref, gate_ref, pg_ref, wbr_ref, wout_ref, o_ref):
    acc = None
    for n, (y_ref, z_ref, g_ref) in enumerate(((ya_ref, za_ref, ga_ref), (yg_ref, zg_ref, gg_ref),
                                                (yr_ref, zr_ref, gr_ref))):
        z = z_ref[...].astype(F32)
        br = (y_ref[...] * (z * _sigmoid(z))).astype(BF16)
        pr = _bdot(br, wbr_ref[n]) * _sigmoid(g_ref[...].astype(F32))
        acc = pr if acc is None else acc + pr
    y = _bdot(acc, wout_ref[...])
    yn = y * lax.rsqrt(jnp.mean(y * y, axis=-1, keepdims=True) + EPS) * pg_ref[...]
    o_ref[...] = x_ref[...] + gate_ref[0] * yn


def _merge(y_mla, y_gm, y_rw, p, x2d, gate, post_g, w_br, w_out, layer, seq):
    T, D = x2d.shape
    tm = 256
    per_b = seq // tm
    rowp = lambda blk: (lambda i: (i, blk))
    yspec = pl.BlockSpec((tm, BW), rowp(0))
    return pl.pallas_call(
        _merge_kernel,
        grid=(T // tm,),
        in_specs=[yspec, yspec, yspec]
        + [pl.BlockSpec((tm, BW), rowp(OFF_Z // BW + n)) for n in range(3)]
        + [pl.BlockSpec((tm, D), rowp(OFF_G // D + n)) for n in range(3)]
        + [
            pl.BlockSpec((tm, D), rowp(0)),
            pl.BlockSpec((1, 1, D), lambda i: (i // per_b, 0, 0)),
            pl.BlockSpec((1, D), lambda i: (0, 0)),
            pl.BlockSpec((None, 3, BW, D), lambda i: (layer, 0, 0, 0)),
            pl.BlockSpec((None, D, D), lambda i: (layer, 0, 0)),
        ],
        out_specs=pl.BlockSpec((tm, D), rowp(0)),
        out_shape=jax.ShapeDtypeStruct((T, D), F32),
        compiler_params=_cparams(("arbitrary",)),
        name="merge",
    )(y_mla, y_gm, y_rw, p, p, p, p, p, p, x2d, gate, post_g, w_br, w_out)


def _relayout_w_in(w_in):
    L, D, _ = w_in.shape
    w = w_in.astype(BF16)
    o_q, o_kv, o_kr, o_gm = 0, MLA_Q_RANK, MLA_Q_RANK + MLA_KV_RANK, MLA_Q_RANK + MLA_KV_RANK + MLA_ROPE
    o_rw = o_gm + 2 * GM_WIDTH
    o_lat = o_rw + 3 * RW_WIDTH
    o_z = o_lat + 2 * RW_LORA
    o_g = o_z + 3 * BW
    half = MLA_ROPE // 2
    kr = w[:, :, o_kr:o_kr + MLA_ROPE]
    krs = jnp.concatenate([kr[:, :, half:], kr[:, :, :half]], axis=-1)
    z64 = jnp.zeros((L, D, MLA_NOPE), BF16)
    z32 = jnp.zeros((L, D, LANE - MLA_NOPE - MLA_ROPE), BF16)
    pad = jnp.zeros((L, D, NP - OFF_LAT - LANE), BF16)
    cols = [
        w[:, :, o_gm:o_lat],
        w[:, :, o_z:o_g],
        w[:, :, o_g:],
        w[:, :, o_q:o_kr],
        z64, kr, z32,
        z64, krs, z32,
        w[:, :, o_lat:o_z],
        pad,
    ]
    return jnp.concatenate(cols, axis=-1)


def _relayout_mla(w_uq, w_ukv):
    L = w_uq.shape[0]
    dq = MLA_NOPE + MLA_ROPE
    half = MLA_ROPE // 2
    wq = w_uq.astype(BF16).reshape(L, MLA_Q_RANK, MLA_HEADS, dq)
    padq = ((0, 0), (0, 0), (0, 0), (0, HEAD_PAD - dq))
    wq_p = jnp.pad(wq, padq).reshape(L, MLA_Q_RANK, MLA_HEADS * HEAD_PAD)
    rope = wq[..., MLA_NOPE:]
    rot = jnp.concatenate([jnp.zeros_like(wq[..., :MLA_NOPE]), rope[..., half:], rope[..., :half]], axis=-1)
    wqr_p = jnp.pad(rot, padq).reshape(L, MLA_Q_RANK, MLA_HEADS * HEAD_PAD)
    wkv = w_ukv.astype(BF16).reshape(L, MLA_KV_RANK, MLA_HEADS, MLA_NOPE + MLA_VDIM)
    wk_p = jnp.pad(wkv[..., :MLA_NOPE], ((0, 0), (0, 0), (0, 0), (0, HEAD_PAD - MLA_NOPE)))
    wk_p = wk_p.reshape(L, MLA_KV_RANK, MLA_HEADS * HEAD_PAD)
    wv = wkv[..., MLA_NOPE:].reshape(L, MLA_KV_RANK, MLA_HEADS // 2, 2, MLA_VDIM)
    zv = jnp.zeros_like(wv[:, :, :, 0])
    wv_p = jnp.concatenate([wv[:, :, :, 0], zv, zv, wv[:, :, :, 1]], axis=-1)
    wv_p = wv_p.reshape(L, MLA_KV_RANK, MLA_HEADS * HEAD_PAD)
    return wq_p, wqr_p, wk_p, wv_p


def kernel(x, c, positions, pre_g, post_g, w_ada, b_ada, w_in, mla_q_norm, mla_w_uq, mla_kv_norm, mla_w_ukv,
           gm_ln_g, gm_ln_b, gm_w_s, gm_b_s, rw_mu, rw_w0, rw_w2, rw_a0, rw_a2, rw_k_k, rw_k_a, rw_r_k,
           rw_lnx_g, rw_lnx_b, rw_v0, rw_v1, rw_v2, w_br, w_out):
    B, S, D = x.shape
    L = w_in.shape[0]
    T = B * S
    x2d = x.reshape(T, D)

    c_pad = jnp.pad(c, ((0, 8 - B), (0, 0)))
    mods = _ada(c_pad, w_ada, b_ada)[:, :B]
    ctab, stab = _rope_tables(positions)

    w_in_p = _relayout_w_in(w_in)
    wq_p, wqr_p, wk_p, wv_p = _relayout_mla(mla_w_uq, mla_w_ukv)
    w_br_b = w_br.astype(BF16)
    w_out_b = w_out.astype(BF16)
    zl = jnp.zeros((RW_LORA, RW_WIDTH), F32)
    pair_lane = np.arange(MLA_HEADS * HEAD_PAD) % (2 * HEAD_PAD)
    v_ones = jnp.asarray(((pair_lane >= MLA_VDIM) & (pair_lane < 2 * HEAD_PAD - MLA_VDIM)).astype(np.float32)[None, :])

    v_first = None
    for l in range(L):
        shift = mods[l, :, :D].reshape(B, 1, D)
        scale = mods[l, :, D:2 * D].reshape(B, 1, D)
        gate = mods[l, :, 2 * D:].reshape(B, 1, D)
        p = _inproj(x2d, scale, shift, pre_g[l].reshape(1, D), w_in_p, l, S)

        q, k, v = _mla_prep(p, ctab, stab, mla_q_norm[l].reshape(1, -1), mla_kv_norm[l].reshape(1, -1),
                            wq_p, wqr_p, wk_p, wv_p, v_ones, l)
        y_mla = _attention(q, k, v, B, S)

        y_gm = _gmlp(p, gm_ln_g[l].reshape(1, -1), gm_ln_b[l].reshape(1, -1), gm_w_s[l], gm_b_s[l])

        mu = rw_mu[l]
        prm = {
            "mu_r": mu[:RW_WIDTH].reshape(1, -1),
            "mu_k": mu[RW_WIDTH:2 * RW_WIDTH].reshape(1, -1),
            "mu_v": mu[2 * RW_WIDTH:3 * RW_WIDTH].reshape(1, -1),
            "mu_l": mu[3 * RW_WIDTH:].reshape(1, -1),
            "w0": rw_w0[l].reshape(1, -1),
            "w2": jnp.concatenate([rw_w2[l], zl], axis=0),
            "a0": rw_a0[l].reshape(1, -1),
            "a2": jnp.concatenate([zl, rw_a2[l]], axis=0),
            "k_k": rw_k_k[l].reshape(1, -1),
            "k_a": rw_k_a[l].reshape(1, -1),
            "r_k": rw_r_k[l].reshape(1, -1),
            "lnx_g": rw_lnx_g[l].reshape(1, -1),
            "lnx_b": rw_lnx_b[l].reshape(1, -1),
        }
        if l > 0:
            prm["v0"] = rw_v0[l - 1].reshape(1, -1)
            prm["v1"] = jnp.pad(rw_v1[l - 1], ((0, 0), (0, LANE - RW_V_LORA)))
            prm["v2"] = jnp.pad(rw_v2[l - 1], ((0, LANE - RW_V_LORA), (0, 0)))
        y_rw, v_first = _rwkv(p, v_first, prm, B, S)

        x2d = _merge(y_mla, y_gm, y_rw, p, x2d, gate, post_g[l].reshape(1, D), w_br_b, w_out_b, l, S)
    return x2d.reshape(B, S, D)
```

```python
import functools
import math

import jax
import jax.numpy as jnp
import numpy as np
from jax import lax
from jax.experimental import pallas as pl
from jax.experimental.pallas import tpu as pltpu

F32 = jnp.float32
BF16 = jnp.bfloat16
HIGHEST = lax.Precision.HIGHEST

CHUNK = 64
EPS = 1e-6
LN_EPS = 1e-5
MLA_HEADS = 8
MLA_Q_RANK = 256
MLA_KV_RANK = 128
MLA_NOPE = 64
MLA_ROPE = 32
MLA_VDIM = 64
ROPE_THETA = 10000.0
GM_GROUPS = 4
GM_GROUP_CH = 128
GM_WIDTH = 512
GM_BLOCK = 128
RW_HEADS = 8
RW_HEAD = 64
RW_WIDTH = 512
RW_LORA = 64
RW_V_LORA = 32
RW_LN_EPS = 64e-5
BW = 512
LANE = 128
HEAD_PAD = 128
ATT_NH = 4

OFF_GM = 0
OFF_R, OFF_K, OFF_V = 1024, 1536, 2048
OFF_Z = 2560
OFF_G = 4096
OFF_QLAT = 7168
OFF_KVLAT = 7424
OFF_KR = 7552
OFF_KRS = 7680
OFF_LAT = 7808
NP = 8192

VMEM_LIMIT = 56 * 1024 * 1024
NT = (((1,), (1,)), ((), ()))
TN = (((0,), (0,)), ((), ()))


def _cparams(sem):
    return pltpu.CompilerParams(dimension_semantics=sem, vmem_limit_bytes=VMEM_LIMIT)


def _mm(a, b):
    return jnp.dot(a, b, preferred_element_type=F32, precision=HIGHEST)


def _bdot(a, b, dims=(((1,), (0,)), ((), ()))):
    return lax.dot_general(a.astype(BF16), b.astype(BF16), dims, preferred_element_type=F32)


def _sigmoid(x):
    return 1.0 / (1.0 + jnp.exp(-x))


def _ada_kernel(c_ref, w_ref, b_ref, o_ref):
    c = c_ref[...]
    ca = c * _sigmoid(c)
    o_ref[0] = _mm(ca, w_ref[0]) + b_ref[0]


def _ada(c_pad, w_ada, b_ada):
    L, D, D3 = w_ada.shape
    tn = 1024
    return pl.pallas_call(
        _ada_kernel,
        grid=(L, D3 // tn),
        in_specs=[
            pl.BlockSpec((8, D), lambda l, j: (0, 0)),
            pl.BlockSpec((1, D, tn), lambda l, j: (l, 0, j)),
            pl.BlockSpec((1, 1, tn), lambda l, j: (l, 0, j)),
        ],
        out_specs=pl.BlockSpec((1, 8, tn), lambda l, j: (l, 0, j)),
        out_shape=jax.ShapeDtypeStruct((L, 8, D3), F32),
        compiler_params=_cparams(("arbitrary", "arbitrary")),
        name="ada",
    )(c_pad, w_ada, b_ada.reshape(L, 1, D3))


def _rope_kernel(pos_ref, freq_ref, sign_ref, c_ref, s_ref):
    pos = pos_ref[...].astype(F32)
    ang = pos * freq_ref[...]
    lane = lax.broadcasted_iota(jnp.int32, ang.shape, 1)
    is_rope = (lane >= MLA_NOPE) & (lane < MLA_NOPE + MLA_ROPE)
    c_ref[...] = jnp.where(is_rope, jnp.cos(ang), jnp.where(lane < MLA_NOPE, 1.0, 0.0))
    s_ref[...] = jnp.sin(ang) * sign_ref[...]


def _rope_tables(positions):
    T = positions.size
    tm = 2048
    inv_freq = ROPE_THETA ** (-np.arange(0, MLA_ROPE, 2, dtype=np.float32) / MLA_ROPE)
    half = MLA_ROPE // 2
    freq = np.zeros((1, LANE), np.float32)
    freq[0, MLA_NOPE:MLA_NOPE + half] = inv_freq
    freq[0, MLA_NOPE + half:MLA_NOPE + MLA_ROPE] = inv_freq
    sign = np.zeros((1, LANE), np.float32)
    sign[0, MLA_NOPE:MLA_NOPE + half] = -1.0
    sign[0, MLA_NOPE + half:MLA_NOPE + MLA_ROPE] = 1.0
    return pl.pallas_call(
        _rope_kernel,
        grid=(T // tm,),
        in_specs=[
            pl.BlockSpec((tm, 1), lambda i: (i, 0)),
            pl.BlockSpec((1, LANE), lambda i: (0, 0)),
            pl.BlockSpec((1, LANE), lambda i: (0, 0)),
        ],
        out_specs=[pl.BlockSpec((tm, LANE), lambda i: (i, 0))] * 2,
        out_shape=[jax.ShapeDtypeStruct((T, LANE), F32)] * 2,
        compiler_params=_cparams(("arbitrary",)),
        name="rope_tables",
    )(positions.reshape(T, 1), jnp.asarray(freq), jnp.asarray(sign))


def _inproj_kernel(x_ref, sc_ref, sh_ref, g_ref, w_ref, o_ref, h_ref):
    @pl.when(pl.program_id(1) == 0)
    def _():
        x = x_ref[...]
        ms = jnp.mean(x * x, axis=-1, keepdims=True)
        y = x * lax.rsqrt(ms + EPS) * g_ref[...]
        h_ref[...] = (y * (1.0 + sc_ref[0]) + sh_ref[0]).astype(BF16)

    o_ref[...] = _bdot(h_ref[...], w_ref[...]).astype(BF16)


def _inproj(x2d, scale, shift, pre_g, w_in_p, layer, seq):
    T, D = x2d.shape
    tm, tn = 1024, 1024
    per_b = seq // tm
    return pl.pallas_call(
        _inproj_kernel,
        grid=(T // tm, NP // tn),
        in_specs=[
            pl.BlockSpec((tm, D), lambda i, j: (i, 0)),
            pl.BlockSpec((1, 1, D), lambda i, j: (i // per_b, 0, 0)),
            pl.BlockSpec((1, 1, D), lambda i, j: (i // per_b, 0, 0)),
            pl.BlockSpec((1, D), lambda i, j: (0, 0)),
            pl.BlockSpec((None, D, tn), lambda i, j: (layer, 0, j)),
        ],
        out_specs=pl.BlockSpec((tm, tn), lambda i, j: (i, j)),
        out_shape=jax.ShapeDtypeStruct((T, NP), BF16),
        scratch_shapes=[pltpu.VMEM((tm, D), BF16)],
        compiler_params=_cparams(("arbitrary", "arbitrary")),
        name="inproj",
    )(x2d, scale, shift, pre_g, w_in_p)


def _mla_prep_kernel(ql_ref, kvl_ref, kr_ref, krs_ref, ct_ref, st_ref, qg_ref, kvg_ref,
                     wq_ref, wqr_ref, wk_ref, wv_ref, vone_ref, q_ref, k_ref, v_ref, *, scale):
    ql = ql_ref[...].astype(F32)
    qn = (ql * lax.rsqrt(jnp.mean(ql * ql, axis=-1, keepdims=True) + EPS) * qg_ref[...]).astype(BF16)
    kvl = kvl_ref[...].astype(F32)
    kvn = (kvl * lax.rsqrt(jnp.mean(kvl * kvl, axis=-1, keepdims=True) + EPS) * kvg_ref[...]).astype(BF16)
    ct = ct_ref[...]
    st = st_ref[...]
    kpe = kr_ref[...].astype(F32) * ct + krs_ref[...].astype(F32) * st
    gw = ATT_NH * HEAD_PAD
    vfull = (_bdot(kvn, wv_ref[...]) + vone_ref[...]).astype(BF16)
    for g in range(MLA_HEADS // ATT_NH):
        v_ref[g] = vfull[:, g * gw:(g + 1) * gw]
    for h in range(MLA_HEADS):
        sl = slice(h * HEAD_PAD, (h + 1) * HEAD_PAD)
        gsl = slice((h % ATT_NH) * HEAD_PAD, (h % ATT_NH + 1) * HEAD_PAD)
        q = _bdot(qn, wq_ref[:, sl]) * ct + _bdot(qn, wqr_ref[:, sl]) * st
        q_ref[h // ATT_NH, :, gsl] = (q * scale).astype(BF16)
        k_ref[h // ATT_NH, :, gsl] = (_bdot(kvn, wk_ref[:, sl]) + kpe).astype(BF16)


def _mla_prep(p, ctab, stab, qg, kvg, wq, wqr, wk, wv, vone, layer):
    T = p.shape[0]
    tm = 512
    HP = MLA_HEADS * HEAD_PAD
    scale = float((MLA_NOPE + MLA_ROPE) ** -0.5) * math.log2(math.e)
    row = lambda blk: (lambda i: (i, blk))
    const = lambda i: (0, 0)
    wspec = lambda rows: pl.BlockSpec((None, rows, HP), lambda i: (layer, 0, 0))
    return pl.pallas_call(
        functools.partial(_mla_prep_kernel, scale=scale),
        grid=(T // tm,),
        in_specs=[
            pl.BlockSpec((tm, MLA_Q_RANK), row(OFF_QLAT // MLA_Q_RANK)),
            pl.BlockSpec((tm, LANE), row(OFF_KVLAT // LANE)),
            pl.BlockSpec((tm, LANE), row(OFF_KR // LANE)),
            pl.BlockSpec((tm, LANE), row(OFF_KRS // LANE)),
            pl.BlockSpec((tm, LANE), row(0)),
            pl.BlockSpec((tm, LANE), row(0)),
            pl.BlockSpec((1, MLA_Q_RANK), const),
            pl.BlockSpec((1, MLA_KV_RANK), const),
            wspec(MLA_Q_RANK),
            wspec(MLA_Q_RANK),
            wspec(MLA_KV_RANK),
            wspec(MLA_KV_RANK),
            pl.BlockSpec((1, HP), const),
        ],
        out_specs=[pl.BlockSpec((MLA_HEADS // ATT_NH, tm, ATT_NH * HEAD_PAD), lambda i: (0, i, 0))] * 3,
        out_shape=[jax.ShapeDtypeStruct((MLA_HEADS // ATT_NH, T, ATT_NH * HEAD_PAD), BF16)] * 3,
        compiler_params=_cparams(("arbitrary",)),
        name="mla_prep",
    )(p, p, p, p, ctab, stab, qg, kvg, wq, wqr, wk, wv, vone)


def _attn_kernel(q_ref, k_ref, v_ref, o_ref, s_ref, p_ref, m_ref, al_ref, acc_ref, *, tq, nh):
    qi = pl.program_id(2)
    m_ref[...] = jnp.full(m_ref.shape, -1e30, F32)
    acc_ref[...] = jnp.zeros(acc_ref.shape, F32)
    visible = (lax.broadcasted_iota(jnp.int32, (tq, tq), 1) // CHUNK
               <= lax.broadcasted_iota(jnp.int32, (tq, tq), 0) // CHUNK)

    def kv_step(ki, masked):
        start = pl.multiple_of(ki * tq, tq)
        for hh in range(nh):
            hsl = slice(hh * HEAD_PAD, (hh + 1) * HEAD_PAD)
            s = _bdot(q_ref[:, hsl], k_ref[pl.ds(start, tq), hsl], NT)
            if masked:
                s = jnp.where(visible, s, -1e30)
            s_ref[hh] = s
            m_old = m_ref[hh]
            m_new = jnp.maximum(m_old, jnp.max(s, axis=-1, keepdims=True))
            al_ref[hh] = jnp.exp2(m_old - m_new)
            m_ref[hh] = m_new
        for hh in range(nh):
            m_new = m_ref[hh]
            for j in range(tq // LANE):
                csl = slice(j * LANE, (j + 1) * LANE)
                p_ref[hh, :, csl] = jnp.exp2(s_ref[hh, :, csl] - m_new).astype(BF16)
        for hh in range(nh):
            vsl = slice(hh * LANE, (hh + 1) * LANE)
            acc_ref[hh] = al_ref[hh] * acc_ref[hh] + _bdot(p_ref[hh], v_ref[pl.ds(start, tq), vsl])

    def full_step(ki, carry):
        kv_step(ki, False)
        return carry

    lax.fori_loop(0, qi, full_step, 0)
    kv_step(qi, True)
    lane = lax.broadcasted_iota(jnp.int32, (tq, LANE), 1)
    for pr in range(nh // 2):
        a0 = acc_ref[2 * pr]
        a1 = acc_ref[2 * pr + 1]
        l0 = a0[:, MLA_VDIM:MLA_VDIM + 1]
        l1 = a1[:, 0:1]
        o_ref[:, pr * LANE:(pr + 1) * LANE] = jnp.where(lane < MLA_VDIM, a0 / l0, a1 / l1)


def _attention(q, k, v, batch, seq):
    T = q.shape[1]
    tq = 512
    nh = ATT_NH
    nq = seq // tq
    return pl.pallas_call(
        functools.partial(_attn_kernel, tq=tq, nh=nh),
        grid=(batch, MLA_HEADS // nh, nq),
        in_specs=[
            pl.BlockSpec((None, tq, nh * HEAD_PAD), lambda b, h, i: (h, b * nq + i, 0)),
            pl.BlockSpec((None, seq, nh * HEAD_PAD), lambda b, h, i: (h, b, 0)),
            pl.BlockSpec((None, seq, nh * HEAD_PAD), lambda b, h, i: (h, b, 0)),
        ],
        out_specs=pl.BlockSpec((tq, nh * MLA_VDIM), lambda b, h, i: (b * nq + i, h)),
        out_shape=jax.ShapeDtypeStruct((T, MLA_HEADS * MLA_VDIM), F32),
        scratch_shapes=[
            pltpu.VMEM((nh, tq, tq), F32),
            pltpu.VMEM((nh, tq, tq), BF16),
            pltpu.VMEM((nh, tq, LANE), F32),
            pltpu.VMEM((nh, tq, LANE), F32),
            pltpu.VMEM((nh, tq, LANE), F32),
        ],
        compiler_params=_cparams(("arbitrary", "arbitrary", "arbitrary")),
        name="attention",
    )(q, k, v)


def _gmlp_kernel(p_ref, g_ref, b_ref, w_ref, bs_ref, o_ref, *, tm):
    x = p_ref[...].astype(F32)
    ge = 0.5 * x * (1.0 + lax.erf(x * (1.0 / math.sqrt(2.0))))
    u = ge[:, :GM_WIDTH]
    v = ge[:, GM_WIDTH:]
    mu = jnp.mean(v, axis=-1, keepdims=True)
    vc = v - mu
    var = jnp.mean(vc * vc, axis=-1, keepdims=True)
    vn = (vc * lax.rsqrt(var + LN_EPS) * g_ref[...] + b_ref[...]).astype(BF16)
    row = lax.broadcasted_iota(jnp.int32, (GM_BLOCK, GM_BLOCK), 0)
    col = lax.broadcasted_iota(jnp.int32, (GM_BLOCK, GM_BLOCK), 1)
    mask = (col // CHUNK) <= (row // CHUNK)
    for g in range(GM_GROUPS):
        w = jnp.where(mask, w_ref[g], 0.0).astype(BF16)
        csl = slice(g * GM_GROUP_CH, (g + 1) * GM_GROUP_CH)
        for blk in range(tm // GM_BLOCK):
            rsl = slice(blk * GM_BLOCK, (blk + 1) * GM_BLOCK)
            s = _bdot(w, vn[rsl, csl]) + bs_ref[g]
            o_ref[rsl, csl] = u[rsl, csl] * s


def _gmlp(p, ln_g, ln_b, w_s, b_s):
    T = p.shape[0]
    tm = 512
    return pl.pallas_call(
        functools.partial(_gmlp_kernel, tm=tm),
        grid=(T // tm,),
        in_specs=[
            pl.BlockSpec((tm, 2 * GM_WIDTH), lambda i: (i, OFF_GM // (2 * GM_WIDTH))),
            pl.BlockSpec((1, GM_WIDTH), lambda i: (0, 0)),
            pl.BlockSpec((1, GM_WIDTH), lambda i: (0, 0)),
            pl.BlockSpec((GM_GROUPS, GM_BLOCK, GM_BLOCK), lambda i: (0, 0, 0)),
            pl.BlockSpec((GM_GROUPS, GM_BLOCK, 1), lambda i: (0, 0, 0)),
        ],
        out_specs=pl.BlockSpec((tm, GM_WIDTH), lambda i: (i, 0)),
        out_shape=jax.ShapeDtypeStruct((T, GM_WIDTH), F32),
        compiler_params=_cparams(("arbitrary",)),
        name="gmlp",
    )(p, ln_g, ln_b, w_s, b_s.reshape(GM_GROUPS, GM_BLOCK, 1))


def _split3_dot_rhs(m_bf16, x):
    x1 = x.astype(BF16)
    r1 = x - x1.astype(F32)
    x2 = r1.astype(BF16)
    x3 = (r1 - x2.astype(F32)).astype(BF16)
    return _bdot(m_bf16, x1) + _bdot(m_bf16, x2) + _bdot(m_bf16, x3)


def _head_sums(x, even):
    s_even = jnp.sum(jnp.where(even, x, 0.0), axis=-1, keepdims=True)
    s_odd = jnp.sum(jnp.where(even, 0.0, x), axis=-1, keepdims=True)
    return jnp.where(even, s_even, s_odd)


def _bd2(x0, x1):
    z = jnp.zeros_like(x0)
    return jnp.concatenate([jnp.concatenate([x0, z], axis=1), jnp.concatenate([z, x1], axis=1)], axis=0)


def _bd(x):
    xb = x.astype(BF16)
    return _bd2(xb[:, :LANE], xb[:, LANE:])


def _unit_lower_inverse(lows, row, colm, n):
    eye = jnp.where(row == colm, 1.0, 0.0)
    base = 16
    same = jnp.where((row // base) == (colm // base), 1.0, 0.0)
    pws = [low * same for low in lows]
    invs = [eye + pw for pw in pws]
    span = 2
    while span < base:
        pws = [_bdot(pw, _bd(pw)) for pw in pws]
        invs = [inv + _bdot(inv, _bd(pw)) for inv, pw in zip(invs, pws)]
        span *= 2
    size = base
    while size < n:
        pair = (row // (2 * size)) == (colm // (2 * size))
        sel = jnp.where(pair, jnp.where((row // size) == (colm // size), 0.0, 1.0), 0.0)
        tmp = [_bdot(inv, _bd(low * sel)) for inv, low in zip(invs, lows)]
        invs = [inv + _bdot(t, _bd(inv)) for inv, t in zip(invs, tmp)]
        size *= 2
    return invs


def _rwkv_kernel(*refs, C, nch, has_vmix):
    if has_vmix:
        (r_ref, k_ref, v_ref, lat_ref, vf_ref, mur_ref, muk_ref, muv_ref, mul_ref, w0_ref, w2_ref, a0_ref, a2_ref,
         kk_ref, ka_ref, v0_ref, v1_ref, v2_ref, rk_ref, g_ref, beta_ref,
         o_ref, cr_ref, ck_ref, cv_ref, cl_ref, s_ref) = refs
    else:
        (r_ref, k_ref, v_ref, lat_ref, mur_ref, muk_ref, muv_ref, mul_ref, w0_ref, w2_ref, a0_ref, a2_ref,
         kk_ref, ka_ref, rk_ref, g_ref, beta_ref,
         o_ref, vo_ref, cr_ref, ck_ref, cv_ref, cl_ref, s_ref) = refs
    P = 2 * C
    R = nch * C

    @pl.when(pl.program_id(1) == 0)
    def _():
        s_ref[...] = jnp.zeros_like(s_ref)
        for c in (cr_ref, ck_ref, cv_ref, cl_ref):
            c[...] = jnp.zeros_like(c)

    def shifted(x_ref, carry_ref, mu_ref):
        x = x_ref[...].astype(F32)
        rolled = pltpu.roll(x, 1, 0)
        rid = lax.broadcasted_iota(jnp.int32, x.shape, 0)
        prev = jnp.where(rid == 0, carry_ref[0:1, :], rolled)
        carry_ref[0:1, :] = x[R - 1:R, :]
        return x + (prev - x) * mu_ref[...]

    r = shifted(r_ref, cr_ref, mur_ref)
    k = shifted(k_ref, ck_ref, muk_ref)
    v = shifted(v_ref, cv_ref, muv_ref)
    lat = shifted(lat_ref, cl_ref, mul_ref)
    ww = w0_ref[...] + _bdot(jnp.tanh(lat), w2_ref[...])
    nw = -ww
    softplus = jnp.maximum(nw, 0.0) + jnp.log(1.0 + jnp.exp(-jnp.abs(nw)))
    lw = -jnp.exp(-softplus - 0.5)
    a = _sigmoid(a0_ref[...] + _bdot(lat, a2_ref[...]))
    if has_vmix:
        gate = _sigmoid(v0_ref[...] + _bdot(_bdot(v, v1_ref[...]), v2_ref[...]))
        v = v + (vf_ref[...] - v) * gate
    else:
        vo_ref[...] = v
    even_r = lax.broadcasted_iota(jnp.int32, (R, LANE), 1) < RW_HEAD
    kk = k * kk_ref[...]
    n2 = jnp.concatenate([_head_sums((kk * kk)[:, j * LANE:(j + 1) * LANE], even_r)
                          for j in range(RW_WIDTH // LANE)], axis=1)
    kk = kk / jnp.maximum(jnp.sqrt(n2), 1e-12)
    k = k * (1.0 + (a - 1.0) * ka_ref[...])
    a_s = -kk
    b_s = kk * a

    rowc = lax.broadcasted_iota(jnp.int32, (C, C), 0)
    colc = lax.broadcasted_iota(jnp.int32, (C, C), 1)
    tri = jnp.where(rowc >= colc, 1.0, 0.0).astype(BF16)
    cum = jnp.concatenate([_split3_dot_rhs(tri, lw[ci * C:(ci + 1) * C, :]) for ci in range(nch)], axis=0)
    e_in = jnp.exp(cum)
    e_inv = jnp.exp(-cum)
    a_t = a_s * jnp.exp(cum - lw)
    r_t = r * e_in
    b_t = b_s * e_inv
    k_t = k * e_inv
    rk = r * k * rk_ref[...]

    row = lax.broadcasted_iota(jnp.int32, (P, 2 * P), 0)
    colm = lax.broadcasted_iota(jnp.int32, (P, 2 * P), 1) % P
    same_head = (row // C) == (colm // C)
    strict = jnp.where(same_head, jnp.where((row % C) > (colm % C), 1.0, 0.0), 0.0)
    incl = jnp.where(same_head, jnp.where((row % C) >= (colm % C), 1.0, 0.0), 0.0)
    even = lax.broadcasted_iota(jnp.int32, (C, LANE), 1) < RW_HEAD

    def stack(x):
        return jnp.concatenate([jnp.where(even, x, 0.0), jnp.where(even, 0.0, x)], axis=0)

    npair = RW_HEADS // 2
    items = [(ci, pr) for ci in range(nch) for pr in range(npair)]
    rws = [slice(ci * C, (ci + 1) * C) for ci, _ in items]
    sls = [slice(pr * LANE, (pr + 1) * LANE) for _, pr in items]
    duals = [(i, i + 1) for i in range(0, len(items), 2)]
    a_ps = [stack(a_t[rw, sl]).astype(BF16) for rw, sl in zip(rws, sls)]
    r_ps = [stack(r_t[rw, sl]) for rw, sl in zip(rws, sls)]
    bks = [jnp.concatenate([stack(b_t[rw, sl]), stack(k_t[rw, sl])], axis=0) for rw, sl in zip(rws, sls)]
    v_ps = [stack(v[rw, sl]).astype(BF16) for rw, sl in zip(rws, sls)]
    gs = [_bdot(jnp.concatenate([ap, rp.astype(BF16)], axis=0), bk, NT)
          for ap, rp, bk in zip(a_ps, r_ps, bks)]
    lows = [jnp.concatenate([gs[i][:P, :P], gs[j][:P, :P]], axis=1) * strict for i, j in duals]
    tinvs = _unit_lower_inverse(lows, row, colm, C)
    akvs = [_bdot(jnp.concatenate([gs[i][:P, P:], gs[j][:P, P:]], axis=1) * strict, _bd2(v_ps[i], v_ps[j]))
            for i, j in duals]
    ws = [_bdot(tinv, _bd2(a_ps[i], a_ps[j])) for tinv, (i, j) in zip(tinvs, duals)]
    u0s = [_bdot(tinv, _bd(akv)) for tinv, akv in zip(tinvs, akvs)]
    rqs = [jnp.concatenate([r_ps[i], r_ps[j]], axis=1)
           + _bdot(jnp.concatenate([gs[i][P:, :P], gs[j][P:, :P]], axis=1) * incl, _bd(w))
           for w, (i, j) in zip(ws, duals)]
    cum_last = [cum[(ci + 1) * C - 1:(ci + 1) * C, :] for ci in range(nch)]
    e_end = [jnp.exp(cum_last[ci] - cum[ci * C:(ci + 1) * C, :]) for ci in range(nch)]
    p_end = [jnp.exp(cl) for cl in cum_last]
    bkes = [jnp.concatenate([stack(b_s[rw, sl] * e_end[ci][:, sl]), stack(k[rw, sl] * e_end[ci][:, sl])],
                            axis=0).astype(BF16)
            for (ci, _), rw, sl in zip(items, rws, sls)]
    half = lambda x, it: x[it // 2][:, (it % 2) * LANE:(it % 2 + 1) * LANE]
    uvs = [jnp.concatenate([half(u0s, it).astype(BF16), v_ps[it]], axis=0) for it in range(len(items))]
    y0s = [_bdot(gs[it][P:, :] * incl, uvs[it]) for it in range(len(items))]
    mts = [_bdot(half(ws, it), bkes[it][:P], TN).astype(BF16) for it in range(len(items))]
    nts = [_bdot(uvs[it], bkes[it], TN) for it in range(len(items))]
    state = [s_ref[pr] for pr in range(npair)]
    ys = []
    for ci in range(nch):
        base = ci * npair
        new_state = []
        for d in range(npair // 2):
            i, j = base + 2 * d, base + 2 * d + 1
            si, sj = state[2 * d], state[2 * d + 1]
            sm = _bdot(jnp.concatenate([si, sj], axis=1), _bd2(mts[i], mts[j]))
            yd = _bdot(rqs[i // 2], _bd2(si.astype(BF16), sj.astype(BF16)), NT)
            for hf, (it, s0) in enumerate(((i, si), (j, sj))):
                hs = slice(hf * LANE, (hf + 1) * LANE)
                new_state.append(s0 * p_end[ci][:, sls[it]] + sm[:, hs] + nts[it])
                y_ps = yd[:, hs] + y0s[it]
                ys.append(y_ps[:C] + y_ps[C:])
        state = new_state
    for pr in range(npair):
        s_ref[pr] = state[pr]
    mus = [_head_sums(y, even) * (1.0 / RW_HEAD) for y in ys]
    ycs = [y - mu for y, mu in zip(ys, mus)]
    vrs = [_head_sums(yc * yc, even) * (1.0 / RW_HEAD) for yc in ycs]
    bns = [_head_sums(rk[rw, sl], even) * v[rw, sl] for rw, sl in zip(rws, sls)]
    for rw, sl, yc, var, bonus in zip(rws, sls, ycs, vrs, bns):
        o_ref[rw, sl] = yc * lax.rsqrt(var + RW_LN_EPS) * g_ref[:, sl] + beta_ref[:, sl] + bonus


def _rwkv(p, v_first, prm, batch, seq):
    T = p.shape[0]
    C = CHUNK
    nch = 4
    R = nch * C
    nt = seq // R
    has_vmix = v_first is not None
    W = RW_WIDTH
    rowp = lambda blk: (lambda b, j: (b * nt + j, blk))
    row0 = lambda b, j: (b * nt + j, 0)
    const = lambda b, j: (0, 0)
    vec = pl.BlockSpec((1, W), const)
    in_specs = [
        pl.BlockSpec((R, W), rowp(OFF_R // W)),
        pl.BlockSpec((R, W), rowp(OFF_K // W)),
        pl.BlockSpec((R, W), rowp(OFF_V // W)),
        pl.BlockSpec((R, LANE), rowp(OFF_LAT // LANE)),
    ]
    args = [p, p, p, p]
    if has_vmix:
        in_specs.append(pl.BlockSpec((R, W), row0))
        args.append(v_first)
    in_specs += [vec, vec, vec, pl.BlockSpec((1, LANE), const), vec, pl.BlockSpec((LANE, W), const),
                 vec, pl.BlockSpec((LANE, W), const), vec, vec]
    args += [prm["mu_r"], prm["mu_k"], prm["mu_v"], prm["mu_l"], prm["w0"], prm["w2"], prm["a0"], prm["a2"],
             prm["k_k"], prm["k_a"]]
    if has_vmix:
        in_specs += [vec, pl.BlockSpec((W, LANE), const), pl.BlockSpec((LANE, W), const)]
        args += [prm["v0"], prm["v1"], prm["v2"]]
    in_specs += [vec, vec, vec]
    args += [prm["r_k"], prm["lnx_g"], prm["lnx_b"]]
    out = pl.BlockSpec((R, W), row0)
    n_out = 1 if has_vmix else 2
    res = pl.pallas_call(
        functools.partial(_rwkv_kernel, C=C, nch=nch, has_vmix=has_vmix),
        grid=(batch, nt),
        in_specs=in_specs,
        out_specs=[out] * n_out,
        out_shape=[jax.ShapeDtypeStruct((T, W), F32)] * n_out,
        scratch_shapes=[pltpu.VMEM((8, W), F32)] * 3 + [pltpu.VMEM((8, LANE), F32),
                                                        pltpu.VMEM((RW_HEADS // 2, LANE, LANE), F32)],
        compiler_params=_cparams(("arbitrary", "arbitrary")),
        name="rwkv7",
    )(*args)
    return (res[0], v_first) if has_vmix else (res[0], res[1])


def _merge_kernel(ya_ref, yg_ref, yr_ref, za_ref, zg_ref, zr_ref, ga_ref, gg_ref, gr_ref,
                  x_ref, gate_ref, pg_ref, wbr_ref, wout_ref, o_ref):
    acc = None
    for n, (y_ref, z_ref, g_ref) in enumerate(((ya_ref, za_ref, ga_ref), (yg_ref, zg_ref, gg_ref),
                                                (yr_ref, zr_ref, gr_ref))):
        z = z_ref[...].astype(F32)
        br = (y_ref[...] * (z * _sigmoid(z))).astype(BF16)
        pr = _bdot(br, wbr_ref[n]) * _sigmoid(g_ref[...].astype(F32))
        acc = pr if acc is None else acc + pr
    y = _bdot(acc, wout_ref[...])
    yn = y * lax.rsqrt(jnp.mean(y * y, axis=-1, keepdims=True) + EPS) * pg_ref[...]
    o_ref[...] = x_ref[...] + gate_ref[0] * yn


def _merge(y_mla, y_gm, y_rw, p, x2d, gate, post_g, w_br, w_out, layer, seq):
    T, D = x2d.shape
    tm = 256
    per_b = seq // tm
    rowp = lambda blk: (lambda i: (i, blk))
    yspec = pl.BlockSpec((tm, BW), rowp(0))
    return pl.pallas_call(
        _merge_kernel,
        grid=(T // tm,),
        in_specs=[yspec, yspec, yspec]
        + [pl.BlockSpec((tm, BW), rowp(OFF_Z // BW + n)) for n in range(3)]
        + [pl.BlockSpec((tm, D), rowp(OFF_G // D + n)) for n in range(3)]
        + [
            pl.BlockSpec((tm, D), rowp(0)),
            pl.BlockSpec((1, 1, D), lambda i: (i // per_b, 0, 0)),
            pl.BlockSpec((1, D), lambda i: (0, 0)),
            pl.BlockSpec((None, 3, BW, D), lambda i: (layer, 0, 0, 0)),
            pl.BlockSpec((None, D, D), lambda i: (layer, 0, 0)),
        ],
        out_specs=pl.BlockSpec((tm, D), rowp(0)),
        out_shape=jax.ShapeDtypeStruct((T, D), F32),
        compiler_params=_cparams(("arbitrary",)),
        name="merge",
    )(y_mla, y_gm, y_rw, p, p, p, p, p, p, x2d, gate, post_g, w_br, w_out)


def _relayout_w_in(w_in):
    L, D, _ = w_in.shape
    w = w_in.astype(BF16)
    o_q, o_kv, o_kr, o_gm = 0, MLA_Q_RANK, MLA_Q_RANK + MLA_KV_RANK, MLA_Q_RANK + MLA_KV_RANK + MLA_ROPE
    o_rw = o_gm + 2 * GM_WIDTH
    o_lat = o_rw + 3 * RW_WIDTH
    o_z = o_lat + 2 * RW_LORA
    o_g = o_z + 3 * BW
    half = MLA_ROPE // 2
    kr = w[:, :, o_kr:o_kr + MLA_ROPE]
    krs = jnp.concatenate([kr[:, :, half:], kr[:, :, :half]], axis=-1)
    z64 = jnp.zeros((L, D, MLA_NOPE), BF16)
    z32 = jnp.zeros((L, D, LANE - MLA_NOPE - MLA_ROPE), BF16)
    pad = jnp.zeros((L, D, NP - OFF_LAT - LANE), BF16)
    cols = [
        w[:, :, o_gm:o_lat],
        w[:, :, o_z:o_g],
        w[:, :, o_g:],
        w[:, :, o_q:o_kr],
        z64, kr, z32,
        z64, krs, z32,
        w[:, :, o_lat:o_z],
        pad,
    ]
    return jnp.concatenate(cols, axis=-1)


def _relayout_mla(w_uq, w_ukv):
    L = w_uq.shape[0]
    dq = MLA_NOPE + MLA_ROPE
    half = MLA_ROPE // 2
    wq = w_uq.astype(BF16).reshape(L, MLA_Q_RANK, MLA_HEADS, dq)
    padq = ((0, 0), (0, 0), (0, 0), (0, HEAD_PAD - dq))
    wq_p = jnp.pad(wq, padq).reshape(L, MLA_Q_RANK, MLA_HEADS * HEAD_PAD)
    rope = wq[..., MLA_NOPE:]
    rot = jnp.concatenate([jnp.zeros_like(wq[..., :MLA_NOPE]), rope[..., half:], rope[..., :half]], axis=-1)
    wqr_p = jnp.pad(rot, padq).reshape(L, MLA_Q_RANK, MLA_HEADS * HEAD_PAD)
    wkv = w_ukv.astype(BF16).reshape(L, MLA_KV_RANK, MLA_HEADS, MLA_NOPE + MLA_VDIM)
    wk_p = jnp.pad(wkv[..., :MLA_NOPE], ((0, 0), (0, 0), (0, 0), (0, HEAD_PAD - MLA_NOPE)))
    wk_p = wk_p.reshape(L, MLA_KV_RANK, MLA_HEADS * HEAD_PAD)
    wv = wkv[..., MLA_NOPE:].reshape(L, MLA_KV_RANK, MLA_HEADS // 2, 2, MLA_VDIM)
    zv = jnp.zeros_like(wv[:, :, :, 0])
    wv_p = jnp.concatenate([wv[:, :, :, 0], zv, zv, wv[:, :, :, 1]], axis=-1)
    wv_p = wv_p.reshape(L, MLA_KV_RANK, MLA_HEADS * HEAD_PAD)
    return wq_p, wqr_p, wk_p, wv_p


def kernel(x, c, positions, pre_g, post_g, w_ada, b_ada, w_in, mla_q_norm, mla_w_uq, mla_kv_norm, mla_w_ukv,
           gm_ln_g, gm_ln_b, gm_w_s, gm_b_s, rw_mu, rw_w0, rw_w2, rw_a0, rw_a2, rw_k_k, rw_k_a, rw_r_k,
           rw_lnx_g, rw_lnx_b, rw_v0, rw_v1, rw_v2, w_br, w_out):
    B, S, D = x.shape
    L = w_in.shape[0]
    T = B * S
    x2d = x.reshape(T, D)

    c_pad = jnp.pad(c, ((0, 8 - B), (0, 0)))
    mods = _ada(c_pad, w_ada, b_ada)[:, :B]
    ctab, stab = _rope_tables(positions)

    w_in_p = _relayout_w_in(w_in)
    wq_p, wqr_p, wk_p, wv_p = _relayout_mla(mla_w_uq, mla_w_ukv)
    w_br_b = w_br.astype(BF16)
    w_out_b = w_out.astype(BF16)
    zl = jnp.zeros((RW_LORA, RW_WIDTH), F32)
    pair_lane = np.arange(MLA_HEADS * HEAD_PAD) % (2 * HEAD_PAD)
    v_ones = jnp.asarray(((pair_lane >= MLA_VDIM) & (pair_lane < 2 * HEAD_PAD - MLA_VDIM)).astype(np.float32)[None, :])

    v_first = None
    for l in range(L):
        shift = mods[l, :, :D].reshape(B, 1, D)
        scale = mods[l, :, D:2 * D].reshape(B, 1, D)
        gate = mods[l, :, 2 * D:].reshape(B, 1, D)
        p = _inproj(x2d, scale, shift, pre_g[l].reshape(1, D), w_in_p, l, S)

        q, k, v = _mla_prep(p, ctab, stab, mla_q_norm[l].reshape(1, -1), mla_kv_norm[l].reshape(1, -1),
                            wq_p, wqr_p, wk_p, wv_p, v_ones, l)
        y_mla = _attention(q, k, v, B, S)

        y_gm = _gmlp(p, gm_ln_g[l].reshape(1, -1), gm_ln_b[l].reshape(1, -1), gm_w_s[l], gm_b_s[l])

        mu = rw_mu[l]
        prm = {
            "mu_r": mu[:RW_WIDTH].reshape(1, -1),
            "mu_k": mu[RW_WIDTH:2 * RW_WIDTH].reshape(1, -1),
            "mu_v": mu[2 * RW_WIDTH:3 * RW_WIDTH].reshape(1, -1),
            "mu_l": mu[3 * RW_WIDTH:].reshape(1, -1),
            "w0": rw_w0[l].reshape(1, -1),
            "w2": jnp.concatenate([rw_w2[l], zl], axis=0),
            "a0": rw_a0[l].reshape(1, -1),
            "a2": jnp.concatenate([zl, rw_a2[l]], axis=0),
            "k_k": rw_k_k[l].reshape(1, -1),
            "k_a": rw_k_a[l].reshape(1, -1),
            "r_k": rw_r_k[l].reshape(1, -1),
            "lnx_g": rw_lnx_g[l].reshape(1, -1),
            "lnx_b": rw_lnx_b[l].reshape(1, -1),
        }
        if l > 0:
            prm["v0"] = rw_v0[l - 1].reshape(1, -1)
            prm["v1"] = jnp.pad(rw_v1[l - 1], ((0, 0), (0, LANE - RW_V_LORA)))
            prm["v2"] = jnp.pad(rw_v2[l - 1], ((0, LANE - RW_V_LORA), (0, 0)))
        y_rw, v_first = _rwkv(p, v_first, prm, B, S)

        x2d = _merge(y_mla, y_gm, y_rw, p, x2d, gate, post_g[l].reshape(1, D), w_br_b, w_out_b, l, S)
    return x2d.reshape(B, S, D)
```

```python
import functools
import math

import jax
import jax.numpy as jnp
import numpy as np
from jax import lax
from jax.experimental import pallas as pl
from jax.experimental.pallas import tpu as pltpu

F32 = jnp.float32
BF16 = jnp.bfloat16
HIGHEST = lax.Precision.HIGHEST

CHUNK = 64
EPS = 1e-6
LN_EPS = 1e-5
MLA_HEADS = 8
MLA_Q_RANK = 256
MLA_KV_RANK = 128
MLA_NOPE = 64
MLA_ROPE = 32
MLA_VDIM = 64
ROPE_THETA = 10000.0
GM_GROUPS = 4
GM_GROUP_CH = 128
GM_WIDTH = 512
GM_BLOCK = 128
RW_HEADS = 8
RW_HEAD = 64
RW_WIDTH = 512
RW_LORA = 64
RW_V_LORA = 32
RW_LN_EPS = 64e-5
BW = 512
LANE = 128
HEAD_PAD = 128
ATT_NH = 4

OFF_GM = 0
OFF_R, OFF_K, OFF_V = 1024, 1536, 2048
OFF_Z = 2560
OFF_G = 4096
OFF_QLAT = 7168
OFF_KVLAT = 7424
OFF_KR = 7552
OFF_KRS = 7680
OFF_LAT = 7808
NP = 8192

VMEM_LIMIT = 56 * 1024 * 1024
NT = (((1,), (1,)), ((), ()))
TN = (((0,), (0,)), ((), ()))


def _cparams(sem):
    return pltpu.CompilerParams(dimension_semantics=sem, vmem_limit_bytes=VMEM_LIMIT)


def _mm(a, b):
    return jnp.dot(a, b, preferred_element_type=F32, precision=HIGHEST)


def _bdot(a, b, dims=(((1,), (0,)), ((), ()))):
    return lax.dot_general(a.astype(BF16), b.astype(BF16), dims, preferred_element_type=F32)


def _sigmoid(x):
    return 1.0 / (1.0 + jnp.exp2(x * (-math.log2(math.e))))


def _ada_kernel(c_ref, w_ref, b_ref, o_ref):
    c = c_ref[...]
    ca = c * _sigmoid(c)
    o_ref[0] = _mm(ca, w_ref[0]) + b_ref[0]


def _ada(c_pad, w_ada, b_ada):
    L, D, D3 = w_ada.shape
    tn = 1024
    return pl.pallas_call(
        _ada_kernel,
        grid=(L, D3 // tn),
        in_specs=[
            pl.BlockSpec((8, D), lambda l, j: (0, 0)),
            pl.BlockSpec((1, D, tn), lambda l, j: (l, 0, j)),
            pl.BlockSpec((1, 1, tn), lambda l, j: (l, 0, j)),
        ],
        out_specs=pl.BlockSpec((1, 8, tn), lambda l, j: (l, 0, j)),
        out_shape=jax.ShapeDtypeStruct((L, 8, D3), F32),
        compiler_params=_cparams(("arbitrary", "arbitrary")),
        name="ada",
    )(c_pad, w_ada, b_ada.reshape(L, 1, D3))


def _rope_kernel(pos_ref, freq_ref, sign_ref, c_ref, s_ref):
    pos = pos_ref[...].astype(F32)
    ang = pos * freq_ref[...]
    lane = lax.broadcasted_iota(jnp.int32, ang.shape, 1)
    is_rope = (lane >= MLA_NOPE) & (lane < MLA_NOPE + MLA_ROPE)
    c_ref[...] = jnp.where(is_rope, jnp.cos(ang), jnp.where(lane < MLA_NOPE, 1.0, 0.0))
    s_ref[...] = jnp.sin(ang) * sign_ref[...]


def _rope_tables(positions):
    T = positions.size
    tm = 2048
    inv_freq = ROPE_THETA ** (-np.arange(0, MLA_ROPE, 2, dtype=np.float32) / MLA_ROPE)
    half = MLA_ROPE // 2
    freq = np.zeros((1, LANE), np.float32)
    freq[0, MLA_NOPE:MLA_NOPE + half] = inv_freq
    freq[0, MLA_NOPE + half:MLA_NOPE + MLA_ROPE] = inv_freq
    sign = np.zeros((1, LANE), np.float32)
    sign[0, MLA_NOPE:MLA_NOPE + half] = -1.0
    sign[0, MLA_NOPE + half:MLA_NOPE + MLA_ROPE] = 1.0
    return pl.pallas_call(
        _rope_kernel,
        grid=(T // tm,),
        in_specs=[
            pl.BlockSpec((tm, 1), lambda i: (i, 0)),
            pl.BlockSpec((1, LANE), lambda i: (0, 0)),
            pl.BlockSpec((1, LANE), lambda i: (0, 0)),
        ],
        out_specs=[pl.BlockSpec((tm, LANE), lambda i: (i, 0))] * 2,
        out_shape=[jax.ShapeDtypeStruct((T, LANE), F32)] * 2,
        compiler_params=_cparams(("arbitrary",)),
        name="rope_tables",
    )(positions.reshape(T, 1), jnp.asarray(freq), jnp.asarray(sign))


def _inproj_kernel(x_ref, sc_ref, sh_ref, g_ref, w_ref, o_ref, h_ref):
    @pl.when(pl.program_id(1) == 0)
    def _():
        x = x_ref[...]
        ms = jnp.mean(x * x, axis=-1, keepdims=True)
        y = x * lax.rsqrt(ms + EPS) * g_ref[...]
        h_ref[...] = (y * (1.0 + sc_ref[0]) + sh_ref[0]).astype(BF16)

    o_ref[...] = _bdot(h_ref[...], w_ref[...], NT).astype(BF16)


def _inproj(x2d, scale, shift, pre_g, w_in_p, layer, seq):
    T, D = x2d.shape
    tm, tn = 1024, 1024
    per_b = seq // tm
    return pl.pallas_call(
        _inproj_kernel,
        grid=(T // tm, NP // tn),
        in_specs=[
            pl.BlockSpec((tm, D), lambda i, j: (i, 0)),
            pl.BlockSpec((1, 1, D), lambda i, j: (i // per_b, 0, 0)),
            pl.BlockSpec((1, 1, D), lambda i, j: (i // per_b, 0, 0)),
            pl.BlockSpec((1, D), lambda i, j: (0, 0)),
            pl.BlockSpec((None, tn, D), lambda i, j: (layer, j, 0)),
        ],
        out_specs=pl.BlockSpec((tm, tn), lambda i, j: (i, j)),
        out_shape=jax.ShapeDtypeStruct((T, NP), BF16),
        scratch_shapes=[pltpu.VMEM((tm, D), BF16)],
        compiler_params=_cparams(("arbitrary", "arbitrary")),
        name="inproj",
    )(x2d, scale, shift, pre_g, w_in_p)


def _mla_prep_kernel(ql_ref, kvl_ref, kr_ref, krs_ref, ct_ref, st_ref, qg_ref, kvg_ref,
                     wq_ref, wqr_ref, wk_ref, wv_ref, vone_ref, q_ref, k_ref, v_ref, *, scale):
    ql = ql_ref[...].astype(F32)
    qn = (ql * lax.rsqrt(jnp.mean(ql * ql, axis=-1, keepdims=True) + EPS) * qg_ref[...]).astype(BF16)
    kvl = kvl_ref[...].astype(F32)
    kvn = (kvl * lax.rsqrt(jnp.mean(kvl * kvl, axis=-1, keepdims=True) + EPS) * kvg_ref[...]).astype(BF16)
    ct = ct_ref[...]
    st = st_ref[...]
    kpe = kr_ref[...].astype(F32) * ct + krs_ref[...].astype(F32) * st
    gw = ATT_NH * HEAD_PAD
    vfull = (_bdot(kvn, wv_ref[...]) + vone_ref[...]).astype(BF16)
    for g in range(MLA_HEADS // ATT_NH):
        v_ref[g] = vfull[:, g * gw:(g + 1) * gw]
    for h in range(MLA_HEADS):
        sl = slice(h * HEAD_PAD, (h + 1) * HEAD_PAD)
        gsl = slice((h % ATT_NH) * HEAD_PAD, (h % ATT_NH + 1) * HEAD_PAD)
        q = _bdot(qn, wq_ref[:, sl]) * ct + _bdot(qn, wqr_ref[:, sl]) * st
        q_ref[h // ATT_NH, :, gsl] = (q * scale).astype(BF16)
        k_ref[h // ATT_NH, :, gsl] = (_bdot(kvn, wk_ref[:, sl]) + kpe).astype(BF16)


def _mla_prep(p, ctab, stab, qg, kvg, wq, wqr, wk, wv, vone, layer):
    T = p.shape[0]
    tm = 512
    HP = MLA_HEADS * HEAD_PAD
    scale = float((MLA_NOPE + MLA_ROPE) ** -0.5) * math.log2(math.e)
    row = lambda blk: (lambda i: (i, blk))
    const = lambda i: (0, 0)
    wspec = lambda rows: pl.BlockSpec((None, rows, HP), lambda i: (layer, 0, 0))
    return pl.pallas_call(
        functools.partial(_mla_prep_kernel, scale=scale),
        grid=(T // tm,),
        in_specs=[
            pl.BlockSpec((tm, MLA_Q_RANK), row(OFF_QLAT // MLA_Q_RANK)),
            pl.BlockSpec((tm, LANE), row(OFF_KVLAT // LANE)),
            pl.BlockSpec((tm, LANE), row(OFF_KR // LANE)),
            pl.BlockSpec((tm, LANE), row(OFF_KRS // LANE)),
            pl.BlockSpec((tm, LANE), row(0)),
            pl.BlockSpec((tm, LANE), row(0)),
            pl.BlockSpec((1, MLA_Q_RANK), const),
            pl.BlockSpec((1, MLA_KV_RANK), const),
            wspec(MLA_Q_RANK),
            wspec(MLA_Q_RANK),
            wspec(MLA_KV_RANK),
            wspec(MLA_KV_RANK),
            pl.BlockSpec((1, HP), const),
        ],
        out_specs=[pl.BlockSpec((MLA_HEADS // ATT_NH, tm, ATT_NH * HEAD_PAD), lambda i: (0, i, 0))] * 3,
        out_shape=[jax.ShapeDtypeStruct((MLA_HEADS // ATT_NH, T, ATT_NH * HEAD_PAD), BF16)] * 3,
        compiler_params=_cparams(("arbitrary",)),
        name="mla_prep",
    )(p, p, p, p, ctab, stab, qg, kvg, wq, wqr, wk, wv, vone)


def _attn_kernel(q_ref, k_ref, v_ref, o_ref, s_ref, p_ref, m_ref, al_ref, acc_ref, *, tq, nh):
    qi = pl.program_id(2)
    m_ref[...] = jnp.full(m_ref.shape, -1e30, F32)
    acc_ref[...] = jnp.zeros(acc_ref.shape, F32)
    visible = (lax.broadcasted_iota(jnp.int32, (tq, tq), 1) // CHUNK
               <= lax.broadcasted_iota(jnp.int32, (tq, tq), 0) // CHUNK)

    def kv_step(ki, masked):
        start = pl.multiple_of(ki * tq, tq)
        for hh in range(nh):
            hsl = slice(hh * HEAD_PAD, (hh + 1) * HEAD_PAD)
            s = _bdot(q_ref[:, hsl], k_ref[pl.ds(start, tq), hsl], NT)
            if masked:
                s = jnp.where(visible, s, -1e30)
            s_ref[hh] = s
            m_old = m_ref[hh]
            m_new = jnp.maximum(m_old, jnp.max(s, axis=-1, keepdims=True))
            al_ref[hh] = jnp.exp2(m_old - m_new)
            m_ref[hh] = m_new
        for hh in range(nh):
            m_new = m_ref[hh]
            for j in range(tq // LANE):
                csl = slice(j * LANE, (j + 1) * LANE)
                p_ref[hh, :, csl] = jnp.exp2(s_ref[hh, :, csl] - m_new).astype(BF16)
        for hh in range(nh):
            vsl = slice(hh * LANE, (hh + 1) * LANE)
            acc_ref[hh] = al_ref[hh] * acc_ref[hh] + _bdot(p_ref[hh], v_ref[pl.ds(start, tq), vsl])

    def full_step(ki, carry):
        kv_step(ki, False)
        return carry

    lax.fori_loop(0, qi, full_step, 0)
    kv_step(qi, True)
    lane = lax.broadcasted_iota(jnp.int32, (tq, LANE), 1)
    for pr in range(nh // 2):
        a0 = acc_ref[2 * pr]
        a1 = acc_ref[2 * pr + 1]
        l0 = a0[:, MLA_VDIM:MLA_VDIM + 1]
        l1 = a1[:, 0:1]
        o_ref[:, pr * LANE:(pr + 1) * LANE] = jnp.where(lane < MLA_VDIM, a0 / l0, a1 / l1)


def _attention(q, k, v, batch, seq):
    T = q.shape[1]
    tq = 512
    nh = ATT_NH
    nq = seq // tq
    return pl.pallas_call(
        functools.partial(_attn_kernel, tq=tq, nh=nh),
        grid=(batch, MLA_HEADS // nh, nq),
        in_specs=[
            pl.BlockSpec((None, tq, nh * HEAD_PAD), lambda b, h, i: (h, b * nq + i, 0)),
            pl.BlockSpec((None, seq, nh * HEAD_PAD), lambda b, h, i: (h, b, 0)),
            pl.BlockSpec((None, seq, nh * HEAD_PAD), lambda b, h, i: (h, b, 0)),
        ],
        out_specs=pl.BlockSpec((tq, nh * MLA_VDIM), lambda b, h, i: (b * nq + i, h)),
        out_shape=jax.ShapeDtypeStruct((T, MLA_HEADS * MLA_VDIM), F32),
        scratch_shapes=[
            pltpu.VMEM((nh, tq, tq), F32),
            pltpu.VMEM((nh, tq, tq), BF16),
            pltpu.VMEM((nh, tq, LANE), F32),
            pltpu.VMEM((nh, tq, LANE), F32),
            pltpu.VMEM((nh, tq, LANE), F32),
        ],
        compiler_params=_cparams(("arbitrary", "arbitrary", "arbitrary")),
        name="attention",
    )(q, k, v)


def _gmlp_kernel(p_ref, g_ref, b_ref, w_ref, bs_ref, o_ref, *, tm):
    x = p_ref[...].astype(F32)
    ge = 0.5 * x * (1.0 + lax.erf(x * (1.0 / math.sqrt(2.0))))
    u = ge[:, :GM_WIDTH]
    v = ge[:, GM_WIDTH:]
    mu = jnp.mean(v, axis=-1, keepdims=True)
    vc = v - mu
    var = jnp.mean(vc * vc, axis=-1, keepdims=True)
    vn = (vc * lax.rsqrt(var + LN_EPS) * g_ref[...] + b_ref[...]).astype(BF16)
    row = lax.broadcasted_iota(jnp.int32, (GM_BLOCK, GM_BLOCK), 0)
    col = lax.broadcasted_iota(jnp.int32, (GM_BLOCK, GM_BLOCK), 1)
    mask = (col // CHUNK) <= (row // CHUNK)
    for g in range(GM_GROUPS):
        w = jnp.where(mask, w_ref[g], 0.0).astype(BF16)
        csl = slice(g * GM_GROUP_CH, (g + 1) * GM_GROUP_CH)
        for blk in range(tm // GM_BLOCK):
            rsl = slice(blk * GM_BLOCK, (blk + 1) * GM_BLOCK)
            s = _bdot(w, vn[rsl, csl]) + bs_ref[g]
            o_ref[rsl, csl] = u[rsl, csl] * s


def _gmlp(p, ln_g, ln_b, w_s, b_s):
    T = p.shape[0]
    tm = 512
    return pl.pallas_call(
        functools.partial(_gmlp_kernel, tm=tm),
        grid=(T // tm,),
        in_specs=[
            pl.BlockSpec((tm, 2 * GM_WIDTH), lambda i: (i, OFF_GM // (2 * GM_WIDTH))),
            pl.BlockSpec((1, GM_WIDTH), lambda i: (0, 0)),
            pl.BlockSpec((1, GM_WIDTH), lambda i: (0, 0)),
            pl.BlockSpec((GM_GROUPS, GM_BLOCK, GM_BLOCK), lambda i: (0, 0, 0)),
            pl.BlockSpec((GM_GROUPS, GM_BLOCK, 1), lambda i: (0, 0, 0)),
        ],
        out_specs=pl.BlockSpec((tm, GM_WIDTH), lambda i: (i, 0)),
        out_shape=jax.ShapeDtypeStruct((T, GM_WIDTH), F32),
        compiler_params=_cparams(("arbitrary",)),
        name="gmlp",
    )(p, ln_g, ln_b, w_s, b_s.reshape(GM_GROUPS, GM_BLOCK, 1))


def _split3_dot_rhs(m_bf16, x):
    x1 = x.astype(BF16)
    r1 = x - x1.astype(F32)
    x2 = r1.astype(BF16)
    x3 = (r1 - x2.astype(F32)).astype(BF16)
    return _bdot(m_bf16, x1) + _bdot(m_bf16, x2) + _bdot(m_bf16, x3)


def _head_sums(x, even):
    s_even = jnp.sum(jnp.where(even, x, 0.0), axis=-1, keepdims=True)
    s_odd = jnp.sum(jnp.where(even, 0.0, x), axis=-1, keepdims=True)
    return jnp.where(even, s_even, s_odd)


def _bd2(x0, x1):
    z = jnp.zeros_like(x0)
    return jnp.concatenate([jnp.concatenate([x0, z], axis=1), jnp.concatenate([z, x1], axis=1)], axis=0)


def _bd(x):
    xb = x.astype(BF16)
    return _bd2(xb[:, :LANE], xb[:, LANE:])


def _unit_lower_inverse(lows, row, colm, n):
    eye = jnp.where(row == colm, 1.0, 0.0)
    base = 16
    same = jnp.where((row // base) == (colm // base), 1.0, 0.0)
    pws = [low * same for low in lows]
    invs = [eye + pw for pw in pws]
    span = 2
    while span < base:
        pws = [_bdot(pw, _bd(pw)) for pw in pws]
        invs = [inv + _bdot(inv, _bd(pw)) for inv, pw in zip(invs, pws)]
        span *= 2
    size = base
    while size < n:
        pair = (row // (2 * size)) == (colm // (2 * size))
        sel = jnp.where(pair, jnp.where((row // size) == (colm // size), 0.0, 1.0), 0.0)
        tmp = [_bdot(inv, _bd(low * sel)) for inv, low in zip(invs, lows)]
        invs = [inv + _bdot(t, _bd(inv)) for inv, t in zip(invs, tmp)]
        size *= 2
    return invs


def _rwkv_kernel(*refs, C, nch, has_vmix):
    if has_vmix:
        (r_ref, k_ref, v_ref, lat_ref, vf_ref, mur_ref, muk_ref, muv_ref, mul_ref, w0_ref, w2_ref, a0_ref, a2_ref,
         kk_ref, ka_ref, v0_ref, v1_ref, v2_ref, rk_ref, g_ref, beta_ref,
         o_ref, cr_ref, ck_ref, cv_ref, cl_ref, s_ref) = refs
    else:
        (r_ref, k_ref, v_ref, lat_ref, mur_ref, muk_ref, muv_ref, mul_ref, w0_ref, w2_ref, a0_ref, a2_ref,
         kk_ref, ka_ref, rk_ref, g_ref, beta_ref,
         o_ref, vo_ref, cr_ref, ck_ref, cv_ref, cl_ref, s_ref) = refs
    P = 2 * C
    R = nch * C

    @pl.when(pl.program_id(1) == 0)
    def _():
        s_ref[...] = jnp.zeros_like(s_ref)
        for c in (cr_ref, ck_ref, cv_ref, cl_ref):
            c[...] = jnp.zeros_like(c)

    def shifted(x_ref, carry_ref, mu_ref):
        x = x_ref[...].astype(F32)
        rolled = pltpu.roll(x, 1, 0)
        rid = lax.broadcasted_iota(jnp.int32, x.shape, 0)
        prev = jnp.where(rid == 0, carry_ref[0:1, :], rolled)
        carry_ref[0:1, :] = x[R - 1:R, :]
        return x + (prev - x) * mu_ref[...]

    r = shifted(r_ref, cr_ref, mur_ref)
    k = shifted(k_ref, ck_ref, muk_ref)
    v = shifted(v_ref, cv_ref, muv_ref)
    lat = shifted(lat_ref, cl_ref, mul_ref)
    ww = w0_ref[...] + _bdot(jnp.tanh(lat), w2_ref[...])
    nw = -ww
    softplus = jnp.maximum(nw, 0.0) + jnp.log(1.0 + jnp.exp(-jnp.abs(nw)))
    lw = -jnp.exp(-softplus - 0.5)
    a = _sigmoid(a0_ref[...] + _bdot(lat, a2_ref[...]))
    if has_vmix:
        gate = _sigmoid(v0_ref[...] + _bdot(_bdot(v, v1_ref[...]), v2_ref[...]))
        v = v + (vf_ref[...] - v) * gate
    else:
        vo_ref[...] = v
    even_r = lax.broadcasted_iota(jnp.int32, (R, LANE), 1) < RW_HEAD
    kk = k * kk_ref[...]
    n2 = jnp.concatenate([_head_sums((kk * kk)[:, j * LANE:(j + 1) * LANE], even_r)
                          for j in range(RW_WIDTH // LANE)], axis=1)
    kk = kk / jnp.maximum(jnp.sqrt(n2), 1e-12)
    k = k * (1.0 + (a - 1.0) * ka_ref[...])
    a_s = -kk
    b_s = kk * a

    rowc = lax.broadcasted_iota(jnp.int32, (C, C), 0)
    colc = lax.broadcasted_iota(jnp.int32, (C, C), 1)
    tri = jnp.where(rowc >= colc, 1.0, 0.0).astype(BF16)
    cum = jnp.concatenate([_split3_dot_rhs(tri, lw[ci * C:(ci + 1) * C, :]) for ci in range(nch)], axis=0)
    e_in = jnp.exp(cum)
    e_inv = jnp.exp(-cum)
    a_t = a_s * jnp.exp(cum - lw)
    r_t = r * e_in
    b_t = b_s * e_inv
    k_t = k * e_inv
    rk = r * k * rk_ref[...]

    row = lax.broadcasted_iota(jnp.int32, (P, 2 * P), 0)
    colm = lax.broadcasted_iota(jnp.int32, (P, 2 * P), 1) % P
    same_head = (row // C) == (colm // C)
    strict = jnp.where(same_head, jnp.where((row % C) > (colm % C), 1.0, 0.0), 0.0)
    incl = jnp.where(same_head, jnp.where((row % C) >= (colm % C), 1.0, 0.0), 0.0)
    even = lax.broadcasted_iota(jnp.int32, (C, LANE), 1) < RW_HEAD

    def stack(x):
        return jnp.concatenate([jnp.where(even, x, 0.0), jnp.where(even, 0.0, x)], axis=0)

    npair = RW_HEADS // 2
    items = [(ci, pr) for ci in range(nch) for pr in range(npair)]
    rws = [slice(ci * C, (ci + 1) * C) for ci, _ in items]
    sls = [slice(pr * LANE, (pr + 1) * LANE) for _, pr in items]
    duals = [(i, i + 1) for i in range(0, len(items), 2)]
    a_ps = [stack(a_t[rw, sl]).astype(BF16) for rw, sl in zip(rws, sls)]
    r_ps = [stack(r_t[rw, sl]) for rw, sl in zip(rws, sls)]
    bks = [jnp.concatenate([stack(b_t[rw, sl]), stack(k_t[rw, sl])], axis=0) for rw, sl in zip(rws, sls)]
    v_ps = [stack(v[rw, sl]).astype(BF16) for rw, sl in zip(rws, sls)]
    gs = [_bdot(jnp.concatenate([ap, rp.astype(BF16)], axis=0), bk, NT)
          for ap, rp, bk in zip(a_ps, r_ps, bks)]
    lows = [jnp.concatenate([gs[i][:P, :P], gs[j][:P, :P]], axis=1) * strict for i, j in duals]
    tinvs = _unit_lower_inverse(lows, row, colm, C)
    akvs = [_bdot(jnp.concatenate([gs[i][:P, P:], gs[j][:P, P:]], axis=1) * strict, _bd2(v_ps[i], v_ps[j]))
            for i, j in duals]
    ws = [_bdot(tinv, _bd2(a_ps[i], a_ps[j])) for tinv, (i, j) in zip(tinvs, duals)]
    u0s = [_bdot(tinv, _bd(akv)) for tinv, akv in zip(tinvs, akvs)]
    rqs = [jnp.concatenate([r_ps[i], r_ps[j]], axis=1)
           + _bdot(jnp.concatenate([gs[i][P:, :P], gs[j][P:, :P]], axis=1) * incl, _bd(w))
           for w, (i, j) in zip(ws, duals)]
    cum_last = [cum[(ci + 1) * C - 1:(ci + 1) * C, :] for ci in range(nch)]
    e_end = [jnp.exp(cum_last[ci] - cum[ci * C:(ci + 1) * C, :]) for ci in range(nch)]
    p_end = [jnp.exp(cl) for cl in cum_last]
    bkes = [jnp.concatenate([stack(b_s[rw, sl] * e_end[ci][:, sl]), stack(k[rw, sl] * e_end[ci][:, sl])],
                            axis=0).astype(BF16)
            for (ci, _), rw, sl in zip(items, rws, sls)]
    half = lambda x, it: x[it // 2][:, (it % 2) * LANE:(it % 2 + 1) * LANE]
    uvs = [jnp.concatenate([half(u0s, it).astype(BF16), v_ps[it]], axis=0) for it in range(len(items))]
    y0s = [_bdot(gs[it][P:, :] * incl, uvs[it]) for it in range(len(items))]
    mts = [_bdot(half(ws, it), bkes[it][:P], TN).astype(BF16) for it in range(len(items))]
    nts = [_bdot(uvs[it], bkes[it], TN) for it in range(len(items))]
    state = [s_ref[pr] for pr in range(npair)]
    ys = []
    for ci in range(nch):
        base = ci * npair
        new_state = []
        for d in range(npair // 2):
            i, j = base + 2 * d, base + 2 * d + 1
            si, sj = state[2 * d], state[2 * d + 1]
            sm = _bdot(jnp.concatenate([si, sj], axis=1), _bd2(mts[i], mts[j]))
            yd = _bdot(rqs[i // 2], _bd2(si.astype(BF16), sj.astype(BF16)), NT)
            for hf, (it, s0) in enumerate(((i, si), (j, sj))):
                hs = slice(hf * LANE, (hf + 1) * LANE)
                new_state.append(s0 * p_end[ci][:, sls[it]] + sm[:, hs] + nts[it])
                y_ps = yd[:, hs] + y0s[it]
                ys.append(y_ps[:C] + y_ps[C:])
        state = new_state
    for pr in range(npair):
        s_ref[pr] = state[pr]
    mus = [_head_sums(y, even) * (1.0 / RW_HEAD) for y in ys]
    ycs = [y - mu for y, mu in zip(ys, mus)]
    vrs = [_head_sums(yc * yc, even) * (1.0 / RW_HEAD) for yc in ycs]
    bns = [_head_sums(rk[rw, sl], even) * v[rw, sl] for rw, sl in zip(rws, sls)]
    for rw, sl, yc, var, bonus in zip(rws, sls, ycs, vrs, bns):
        o_ref[rw, sl] = yc * lax.rsqrt(var + RW_LN_EPS) * g_ref[:, sl] + beta_ref[:, sl] + bonus


def _rwkv(p, v_first, prm, batch, seq):
    T = p.shape[0]
    C = CHUNK
    nch = 4
    R = nch * C
    nt = seq // R
    has_vmix = v_first is not None
    W = RW_WIDTH
    rowp = lambda blk: (lambda b, j: (b * nt + j, blk))
    row0 = lambda b, j: (b * nt + j, 0)
    const = lambda b, j: (0, 0)
    vec = pl.BlockSpec((1, W), const)
    in_specs = [
        pl.BlockSpec((R, W), rowp(OFF_R // W)),
        pl.BlockSpec((R, W), rowp(OFF_K // W)),
        pl.BlockSpec((R, W), rowp(OFF_V // W)),
        pl.BlockSpec((R, LANE), rowp(OFF_LAT // LANE)),
    ]
    args = [p, p, p, p]
    if has_vmix:
        in_specs.append(pl.BlockSpec((R, W), row0))
        args.append(v_first)
    in_specs += [vec, vec, vec, pl.BlockSpec((1, LANE), const), vec, pl.BlockSpec((LANE, W), const),
                 vec, pl.BlockSpec((LANE, W), const), vec, vec]
    args += [prm["mu_r"], prm["mu_k"], prm["mu_v"], prm["mu_l"], prm["w0"], prm["w2"], prm["a0"], prm["a2"],
             prm["k_k"], prm["k_a"]]
    if has_vmix:
        in_specs += [vec, pl.BlockSpec((W, LANE), const), pl.BlockSpec((LANE, W), const)]
        args += [prm["v0"], prm["v1"], prm["v2"]]
    in_specs += [vec, vec, vec]
    args += [prm["r_k"], prm["lnx_g"], prm["lnx_b"]]
    out = pl.BlockSpec((R, W), row0)
    n_out = 1 if has_vmix else 2
    res = pl.pallas_call(
        functools.partial(_rwkv_kernel, C=C, nch=nch, has_vmix=has_vmix),
        grid=(batch, nt),
        in_specs=in_specs,
        out_specs=[out] * n_out,
        out_shape=[jax.ShapeDtypeStruct((T, W), F32)] * n_out,
        scratch_shapes=[pltpu.VMEM((8, W), F32)] * 3 + [pltpu.VMEM((8, LANE), F32),
                                                        pltpu.VMEM((RW_HEADS // 2, LANE, LANE), F32)],
        compiler_params=_cparams(("arbitrary", "arbitrary")),
        name="rwkv7",
    )(*args)
    return (res[0], v_first) if has_vmix else (res[0], res[1])


def _merge_kernel(ya_ref, yg_ref, yr_ref, za_ref, zg_ref, zr_ref, ga_ref, gg_ref, gr_ref,
                  x_ref, gate_ref, pg_ref, wbr_ref, wout_ref, o_ref):
    acc = None
    for n, (y_ref, z_ref, g_ref) in enumerate(((ya_ref, za_ref, ga_ref), (yg_ref, zg_ref, gg_ref),
                                                (yr_ref, zr_ref, gr_ref))):
        z = z_ref[...].astype(F32)
        br = (y_ref[...] * (z * _sigmoid(z))).astype(BF16)
        pr = _bdot(br, wbr_ref[n]) * _sigmoid(g_ref[...].astype(F32))
        acc = pr if acc is None else acc + pr
    y = _bdot(acc, wout_ref[...])
    yn = y * lax.rsqrt(jnp.mean(y * y, axis=-1, keepdims=True) + EPS) * pg_ref[...]
    o_ref[...] = x_ref[...] + gate_ref[0] * yn


def _merge(y_mla, y_gm, y_rw, p, x2d, gate, post_g, w_br, w_out, layer, seq):
    T, D = x2d.shape
    tm = 256
    per_b = seq // tm
    rowp = lambda blk: (lambda i: (i, blk))
    yspec = pl.BlockSpec((tm, BW), rowp(0))
    return pl.pallas_call(
        _merge_kernel,
        grid=(T // tm,),
        in_specs=[yspec, yspec, yspec]
        + [pl.BlockSpec((tm, BW), rowp(OFF_Z // BW + n)) for n in range(3)]
        + [pl.BlockSpec((tm, D), rowp(OFF_G // D + n)) for n in range(3)]
        + [
            pl.BlockSpec((tm, D), rowp(0)),
            pl.BlockSpec((1, 1, D), lambda i: (i // per_b, 0, 0)),
            pl.BlockSpec((1, D), lambda i: (0, 0)),
            pl.BlockSpec((None, 3, BW, D), lambda i: (layer, 0, 0, 0)),
            pl.BlockSpec((None, D, D), lambda i: (layer, 0, 0)),
        ],
        out_specs=pl.BlockSpec((tm, D), rowp(0)),
        out_shape=jax.ShapeDtypeStruct((T, D), F32),
        compiler_params=_cparams(("arbitrary",)),
        name="merge",
    )(y_mla, y_gm, y_rw, p, p, p, p, p, p, x2d, gate, post_g, w_br, w_out)


def _relayout_w_in(w_in):
    L, D, _ = w_in.shape
    w = jnp.swapaxes(w_in, 1, 2).astype(BF16)
    o_q, o_kv, o_kr, o_gm = 0, MLA_Q_RANK, MLA_Q_RANK + MLA_KV_RANK, MLA_Q_RANK + MLA_KV_RANK + MLA_ROPE
    o_rw = o_gm + 2 * GM_WIDTH
    o_lat = o_rw + 3 * RW_WIDTH
    o_z = o_lat + 2 * RW_LORA
    o_g = o_z + 3 * BW
    half = MLA_ROPE // 2
    kr = w[:, o_kr:o_kr + MLA_ROPE]
    krs = jnp.concatenate([kr[:, half:], kr[:, :half]], axis=1)
    z64 = jnp.zeros((L, MLA_NOPE, D), BF16)
    z32 = jnp.zeros((L, LANE - MLA_NOPE - MLA_ROPE, D), BF16)
    pad = jnp.zeros((L, NP - OFF_LAT - LANE, D), BF16)
    rows = [
        w[:, o_gm:o_lat],
        w[:, o_z:o_g],
        w[:, o_g:],
        w[:, o_q:o_kr],
        z64, kr, z32,
        z64, krs, z32,
        w[:, o_lat:o_z],
        pad,
    ]
    return jnp.concatenate(rows, axis=1)


def _relayout_mla(w_uq, w_ukv):
    L = w_uq.shape[0]
    dq = MLA_NOPE + MLA_ROPE
    half = MLA_ROPE // 2
    wq = w_uq.astype(BF16).reshape(L, MLA_Q_RANK, MLA_HEADS, dq)
    padq = ((0, 0), (0, 0), (0, 0), (0, HEAD_PAD - dq))
    wq_p = jnp.pad(wq, padq).reshape(L, MLA_Q_RANK, MLA_HEADS * HEAD_PAD)
    rope = wq[..., MLA_NOPE:]
    rot = jnp.concatenate([jnp.zeros_like(wq[..., :MLA_NOPE]), rope[..., half:], rope[..., :half]], axis=-1)
    wqr_p = jnp.pad(rot, padq).reshape(L, MLA_Q_RANK, MLA_HEADS * HEAD_PAD)
    wkv = w_ukv.astype(BF16).reshape(L, MLA_KV_RANK, MLA_HEADS, MLA_NOPE + MLA_VDIM)
    wk_p = jnp.pad(wkv[..., :MLA_NOPE], ((0, 0), (0, 0), (0, 0), (0, HEAD_PAD - MLA_NOPE)))
    wk_p = wk_p.reshape(L, MLA_KV_RANK, MLA_HEADS * HEAD_PAD)
    wv = wkv[..., MLA_NOPE:].reshape(L, MLA_KV_RANK, MLA_HEADS // 2, 2, MLA_VDIM)
    zv = jnp.zeros_like(wv[:, :, :, 0])
    wv_p = jnp.concatenate([wv[:, :, :, 0], zv, zv, wv[:, :, :, 1]], axis=-1)
    wv_p = wv_p.reshape(L, MLA_KV_RANK, MLA_HEADS * HEAD_PAD)
    return wq_p, wqr_p, wk_p, wv_p


def kernel(x, c, positions, pre_g, post_g, w_ada, b_ada, w_in, mla_q_norm, mla_w_uq, mla_kv_norm, mla_w_ukv,
           gm_ln_g, gm_ln_b, gm_w_s, gm_b_s, rw_mu, rw_w0, rw_w2, rw_a0, rw_a2, rw_k_k, rw_k_a, rw_r_k,
           rw_lnx_g, rw_lnx_b, rw_v0, rw_v1, rw_v2, w_br, w_out):
    B, S, D = x.shape
    L = w_in.shape[0]
    T = B * S
    x2d = x.reshape(T, D)

    c_pad = jnp.pad(c, ((0, 8 - B), (0, 0)))
    mods = _ada(c_pad, w_ada, b_ada)[:, :B]
    ctab, stab = _rope_tables(positions)

    w_in_p = _relayout_w_in(w_in)
    wq_p, wqr_p, wk_p, wv_p = _relayout_mla(mla_w_uq, mla_w_ukv)
    w_br_b = w_br.astype(BF16)
    w_out_b = w_out.astype(BF16)
    zl = jnp.zeros((RW_LORA, RW_WIDTH), F32)
    pair_lane = np.arange(MLA_HEADS * HEAD_PAD) % (2 * HEAD_PAD)
    v_ones = jnp.asarray(((pair_lane >= MLA_VDIM) & (pair_lane < 2 * HEAD_PAD - MLA_VDIM)).astype(np.float32)[None, :])

    v_first = None
    for l in range(L):
        shift = mods[l, :, :D].reshape(B, 1, D)
        scale = mods[l, :, D:2 * D].reshape(B, 1, D)
        gate = mods[l, :, 2 * D:].reshape(B, 1, D)
        p = _inproj(x2d, scale, shift, pre_g[l].reshape(1, D), w_in_p, l, S)

        q, k, v = _mla_prep(p, ctab, stab, mla_q_norm[l].reshape(1, -1), mla_kv_norm[l].reshape(1, -1),
                            wq_p, wqr_p, wk_p, wv_p, v_ones, l)
        y_mla = _attention(q, k, v, B, S)

        y_gm = _gmlp(p, gm_ln_g[l].reshape(1, -1), gm_ln_b[l].reshape(1, -1), gm_w_s[l], gm_b_s[l])

        mu = rw_mu[l]
        prm = {
            "mu_r": mu[:RW_WIDTH].reshape(1, -1),
            "mu_k": mu[RW_WIDTH:2 * RW_WIDTH].reshape(1, -1),
            "mu_v": mu[2 * RW_WIDTH:3 * RW_WIDTH].reshape(1, -1),
            "mu_l": mu[3 * RW_WIDTH:].reshape(1, -1),
            "w0": rw_w0[l].reshape(1, -1),
            "w2": jnp.concatenate([rw_w2[l], zl], axis=0),
            "a0": rw_a0[l].reshape(1, -1),
            "a2": jnp.concatenate([zl, rw_a2[l]], axis=0),
            "k_k": rw_k_k[l].reshape(1, -1),
            "k_a": rw_k_a[l].reshape(1, -1),
            "r_k": rw_r_k[l].reshape(1, -1),
            "lnx_g": rw_lnx_g[l].reshape(1, -1),
            "lnx_b": rw_lnx_b[l].reshape(1, -1),
        }
        if l > 0:
            prm["v0"] = rw_v0[l - 1].reshape(1, -1)
            prm["v1"] = jnp.pad(rw_v1[l - 1], ((0, 0), (0, LANE - RW_V_LORA)))
            prm["v2"] = jnp.pad(rw_v2[l - 1], ((0, LANE - RW_V_LORA), (0, 0)))
        y_rw, v_first = _rwkv(p, v_first, prm, B, S)

        x2d = _merge(y_mla, y_gm, y_rw, p, x2d, gate, post_g[l].reshape(1, D), w_br_b, w_out_b, l, S)
    return x2d.reshape(B, S, D)
```

```python
import functools
import math

import jax
import jax.numpy as jnp
import numpy as np
from jax import lax
from jax.experimental import pallas as pl
from jax.experimental.pallas import tpu as pltpu

F32 = jnp.float32
BF16 = jnp.bfloat16

CHUNK = 64
EPS = 1e-6
LN_EPS = 1e-5
MLA_HEADS = 8
MLA_Q_RANK = 256
MLA_KV_RANK = 128
MLA_NOPE = 64
MLA_ROPE = 32
MLA_VDIM = 64
ROPE_THETA = 10000.0
GM_GROUPS = 4
GM_GROUP_CH = 128
GM_WIDTH = 512
GM_BLOCK = 128
RW_HEADS = 8
RW_HEAD = 64
RW_WIDTH = 512
RW_LORA = 64
RW_V_LORA = 32
RW_LN_EPS = 64e-5
BW = 512
LANE = 128
HEAD_PAD = 128
ATT_NH = 4

OFF_GM = 0
OFF_R, OFF_K, OFF_V = 1024, 1536, 2048
OFF_Z = 2560
OFF_G = 4096
OFF_QLAT = 7168
OFF_KVLAT = 7424
OFF_KR = 7552
OFF_KRS = 7680
OFF_LAT = 7808
NP = 8192

VMEM_LIMIT = 56 * 1024 * 1024
NT = (((1,), (1,)), ((), ()))
TN = (((0,), (0,)), ((), ()))


def _cparams(sem):
    return pltpu.CompilerParams(dimension_semantics=sem, vmem_limit_bytes=VMEM_LIMIT)


def _bdot(a, b, dims=(((1,), (0,)), ((), ()))):
    return lax.dot_general(a.astype(BF16), b.astype(BF16), dims, preferred_element_type=F32)


def _sigmoid(x):
    return 1.0 / (1.0 + jnp.exp2(x * (-math.log2(math.e))))


def _ada_kernel(c_ref, w_ref, b_ref, o_ref):
    c = c_ref[...]
    ca = c * _sigmoid(c)
    o_ref[0] = _bdot(ca, w_ref[0]) + b_ref[0]


def _ada(c_pad, w_ada, b_ada):
    L, D, D3 = w_ada.shape
    tn = 1024
    return pl.pallas_call(
        _ada_kernel,
        grid=(L, D3 // tn),
        in_specs=[
            pl.BlockSpec((8, D), lambda l, j: (0, 0)),
            pl.BlockSpec((1, D, tn), lambda l, j: (l, 0, j)),
            pl.BlockSpec((1, 1, tn), lambda l, j: (l, 0, j)),
        ],
        out_specs=pl.BlockSpec((1, 8, tn), lambda l, j: (l, 0, j)),
        out_shape=jax.ShapeDtypeStruct((L, 8, D3), F32),
        compiler_params=_cparams(("arbitrary", "arbitrary")),
        name="ada",
    )(c_pad, w_ada, b_ada.reshape(L, 1, D3))


def _rope_kernel(pos_ref, freq_ref, sign_ref, c_ref, s_ref):
    pos = pos_ref[...].astype(F32)
    ang = pos * freq_ref[...]
    lane = lax.broadcasted_iota(jnp.int32, ang.shape, 1)
    is_rope = (lane >= MLA_NOPE) & (lane < MLA_NOPE + MLA_ROPE)
    c_ref[...] = jnp.where(is_rope, jnp.cos(ang), jnp.where(lane < MLA_NOPE, 1.0, 0.0))
    s_ref[...] = jnp.sin(ang) * sign_ref[...]


def _rope_tables(positions):
    T = positions.size
    tm = 2048
    inv_freq = ROPE_THETA ** (-np.arange(0, MLA_ROPE, 2, dtype=np.float32) / MLA_ROPE)
    half = MLA_ROPE // 2
    freq = np.zeros((1, LANE), np.float32)
    freq[0, MLA_NOPE:MLA_NOPE + half] = inv_freq
    freq[0, MLA_NOPE + half:MLA_NOPE + MLA_ROPE] = inv_freq
    sign = np.zeros((1, LANE), np.float32)
    sign[0, MLA_NOPE:MLA_NOPE + half] = -1.0
    sign[0, MLA_NOPE + half:MLA_NOPE + MLA_ROPE] = 1.0
    return pl.pallas_call(
        _rope_kernel,
        grid=(T // tm,),
        in_specs=[
            pl.BlockSpec((tm, 1), lambda i: (i, 0)),
            pl.BlockSpec((1, LANE), lambda i: (0, 0)),
            pl.BlockSpec((1, LANE), lambda i: (0, 0)),
        ],
        out_specs=[pl.BlockSpec((tm, LANE), lambda i: (i, 0))] * 2,
        out_shape=[jax.ShapeDtypeStruct((T, LANE), F32)] * 2,
        compiler_params=_cparams(("arbitrary",)),
        name="rope_tables",
    )(positions.reshape(T, 1), jnp.asarray(freq), jnp.asarray(sign))


def _inproj_kernel(x_ref, sc_ref, sh_ref, g_ref, w_ref, o_ref, h_ref):
    @pl.when(pl.program_id(1) == 0)
    def _():
        x = x_ref[...]
        ms = jnp.mean(x * x, axis=-1, keepdims=True)
        y = x * lax.rsqrt(ms + EPS) * g_ref[...]
        h_ref[...] = (y * (1.0 + sc_ref[0]) + sh_ref[0]).astype(BF16)

    o_ref[...] = _bdot(h_ref[...], w_ref[...], NT).astype(BF16)


def _inproj(x2d, scale, shift, pre_g, w_in_p, layer, seq):
    T, D = x2d.shape
    tm, tn = 1024, 2048
    per_b = seq // tm
    return pl.pallas_call(
        _inproj_kernel,
        grid=(T // tm, NP // tn),
        in_specs=[
            pl.BlockSpec((tm, D), lambda i, j: (i, 0)),
            pl.BlockSpec((1, 1, D), lambda i, j: (i // per_b, 0, 0)),
            pl.BlockSpec((1, 1, D), lambda i, j: (i // per_b, 0, 0)),
            pl.BlockSpec((1, D), lambda i, j: (0, 0)),
            pl.BlockSpec((None, tn, D), lambda i, j: (layer, j, 0)),
        ],
        out_specs=pl.BlockSpec((tm, tn), lambda i, j: (i, j)),
        out_shape=jax.ShapeDtypeStruct((T, NP), BF16),
        scratch_shapes=[pltpu.VMEM((tm, D), BF16)],
        compiler_params=_cparams(("arbitrary", "arbitrary")),
        name="inproj",
    )(x2d, scale, shift, pre_g, w_in_p)


def _mla_prep_kernel(ql_ref, kvl_ref, kr_ref, krs_ref, ct_ref, st_ref, qg_ref, kvg_ref,
                     wq_ref, wqr_ref, wk_ref, wv_ref, vone_ref, q_ref, k_ref, v_ref, *, scale):
    ql = ql_ref[...].astype(F32)
    qn = (ql * lax.rsqrt(jnp.mean(ql * ql, axis=-1, keepdims=True) + EPS) * qg_ref[...]).astype(BF16)
    kvl = kvl_ref[...].astype(F32)
    kvn = (kvl * lax.rsqrt(jnp.mean(kvl * kvl, axis=-1, keepdims=True) + EPS) * kvg_ref[...]).astype(BF16)
    ct = ct_ref[...]
    st = st_ref[...]
    kpe = kr_ref[...].astype(F32) * ct + krs_ref[...].astype(F32) * st
    gw = ATT_NH * HEAD_PAD
    vfull = (_bdot(kvn, wv_ref[...]) + vone_ref[...]).astype(BF16)
    for g in range(MLA_HEADS // ATT_NH):
        v_ref[g] = vfull[:, g * gw:(g + 1) * gw]
    for h in range(MLA_HEADS):
        sl = slice(h * HEAD_PAD, (h + 1) * HEAD_PAD)
        gsl = slice((h % ATT_NH) * HEAD_PAD, (h % ATT_NH + 1) * HEAD_PAD)
        q = _bdot(qn, wq_ref[:, sl]) * ct + _bdot(qn, wqr_ref[:, sl]) * st
        q_ref[h // ATT_NH, :, gsl] = (q * scale).astype(BF16)
        k_ref[h // ATT_NH, :, gsl] = (_bdot(kvn, wk_ref[:, sl]) + kpe).astype(BF16)


def _mla_prep(p, ctab, stab, qg, kvg, wq, wqr, wk, wv, vone, layer):
    T = p.shape[0]
    tm = 512
    HP = MLA_HEADS * HEAD_PAD
    scale = float((MLA_NOPE + MLA_ROPE) ** -0.5) * math.log2(math.e)
    row = lambda blk: (lambda i: (i, blk))
    const = lambda i: (0, 0)
    wspec = lambda rows: pl.BlockSpec((None, rows, HP), lambda i: (layer, 0, 0))
    return pl.pallas_call(
        functools.partial(_mla_prep_kernel, scale=scale),
        grid=(T // tm,),
        in_specs=[
            pl.BlockSpec((tm, MLA_Q_RANK), row(OFF_QLAT // MLA_Q_RANK)),
            pl.BlockSpec((tm, LANE), row(OFF_KVLAT // LANE)),
            pl.BlockSpec((tm, LANE), row(OFF_KR // LANE)),
            pl.BlockSpec((tm, LANE), row(OFF_KRS // LANE)),
            pl.BlockSpec((tm, LANE), row(0)),
            pl.BlockSpec((tm, LANE), row(0)),
            pl.BlockSpec((1, MLA_Q_RANK), const),
            pl.BlockSpec((1, MLA_KV_RANK), const),
            wspec(MLA_Q_RANK),
            wspec(MLA_Q_RANK),
            wspec(MLA_KV_RANK),
            wspec(MLA_KV_RANK),
            pl.BlockSpec((1, HP), const),
        ],
        out_specs=[pl.BlockSpec((MLA_HEADS // ATT_NH, tm, ATT_NH * HEAD_PAD), lambda i: (0, i, 0))] * 3,
        out_shape=[jax.ShapeDtypeStruct((MLA_HEADS // ATT_NH, T, ATT_NH * HEAD_PAD), BF16)] * 3,
        compiler_params=_cparams(("arbitrary",)),
        name="mla_prep",
    )(p, p, p, p, ctab, stab, qg, kvg, wq, wqr, wk, wv, vone)


def _attn_kernel(q_ref, k_ref, v_ref, o_ref, s_ref, p_ref, m_ref, al_ref, acc_ref, *, tq, nh):
    qi = pl.program_id(2)
    m_ref[...] = jnp.full(m_ref.shape, -1e30, F32)
    acc_ref[...] = jnp.zeros(acc_ref.shape, F32)
    visible = (lax.broadcasted_iota(jnp.int32, (tq, tq), 1) // CHUNK
               <= lax.broadcasted_iota(jnp.int32, (tq, tq), 0) // CHUNK)

    def kv_step(ki, masked):
        start = pl.multiple_of(ki * tq, tq)
        for hh in range(nh):
            hsl = slice(hh * HEAD_PAD, (hh + 1) * HEAD_PAD)
            s = _bdot(q_ref[:, hsl], k_ref[pl.ds(start, tq), hsl], NT)
            if masked:
                s = jnp.where(visible, s, -1e30)
            s_ref[hh] = s
            m_old = m_ref[hh]
            m_new = jnp.maximum(m_old, jnp.max(s, axis=-1, keepdims=True))
            al_ref[hh] = jnp.exp2(m_old - m_new)
            m_ref[hh] = m_new
        for hh in range(nh):
            m_new = m_ref[hh]
            for j in range(tq // LANE):
                csl = slice(j * LANE, (j + 1) * LANE)
                p_ref[hh, :, csl] = jnp.exp2(s_ref[hh, :, csl] - m_new).astype(BF16)
        for hh in range(nh):
            vsl = slice(hh * LANE, (hh + 1) * LANE)
            acc_ref[hh] = al_ref[hh] * acc_ref[hh] + _bdot(p_ref[hh], v_ref[pl.ds(start, tq), vsl])

    def full_step(ki, carry):
        kv_step(ki, False)
        return carry

    lax.fori_loop(0, qi, full_step, 0)
    kv_step(qi, True)
    lane = lax.broadcasted_iota(jnp.int32, (tq, LANE), 1)
    for pr in range(nh // 2):
        a0 = acc_ref[2 * pr]
        a1 = acc_ref[2 * pr + 1]
        l0 = a0[:, MLA_VDIM:MLA_VDIM + 1]
        l1 = a1[:, 0:1]
        o_ref[:, pr * LANE:(pr + 1) * LANE] = jnp.where(lane < MLA_VDIM, a0 / l0, a1 / l1)


def _attention(q, k, v, batch, seq):
    T = q.shape[1]
    tq = 512
    nh = ATT_NH
    nq = seq // tq
    return pl.pallas_call(
        functools.partial(_attn_kernel, tq=tq, nh=nh),
        grid=(batch, MLA_HEADS // nh, nq),
        in_specs=[
            pl.BlockSpec((None, tq, nh * HEAD_PAD), lambda b, h, i: (h, b * nq + i, 0)),
            pl.BlockSpec((None, seq, nh * HEAD_PAD), lambda b, h, i: (h, b, 0)),
            pl.BlockSpec((None, seq, nh * HEAD_PAD), lambda b, h, i: (h, b, 0)),
        ],
        out_specs=pl.BlockSpec((tq, nh * MLA_VDIM), lambda b, h, i: (b * nq + i, h)),
        out_shape=jax.ShapeDtypeStruct((T, MLA_HEADS * MLA_VDIM), F32),
        scratch_shapes=[
            pltpu.VMEM((nh, tq, tq), F32),
            pltpu.VMEM((nh, tq, tq), BF16),
            pltpu.VMEM((nh, tq, LANE), F32),
            pltpu.VMEM((nh, tq, LANE), F32),
            pltpu.VMEM((nh, tq, LANE), F32),
        ],
        compiler_params=_cparams(("arbitrary", "arbitrary", "arbitrary")),
        name="attention",
    )(q, k, v)


def _gmlp_kernel(p_ref, g_ref, b_ref, w_ref, bs_ref, o_ref, *, tm):
    x = p_ref[...].astype(F32)
    ge = 0.5 * x * (1.0 + lax.erf(x * (1.0 / math.sqrt(2.0))))
    u = ge[:, :GM_WIDTH]
    v = ge[:, GM_WIDTH:]
    mu = jnp.mean(v, axis=-1, keepdims=True)
    vc = v - mu
    var = jnp.mean(vc * vc, axis=-1, keepdims=True)
    vn = (vc * lax.rsqrt(var + LN_EPS) * g_ref[...] + b_ref[...]).astype(BF16)
    row = lax.broadcasted_iota(jnp.int32, (GM_BLOCK, GM_BLOCK), 0)
    col = lax.broadcasted_iota(jnp.int32, (GM_BLOCK, GM_BLOCK), 1)
    mask = (col // CHUNK) <= (row // CHUNK)
    for g in range(GM_GROUPS):
        w = jnp.where(mask, w_ref[g], 0.0).astype(BF16)
        csl = slice(g * GM_GROUP_CH, (g + 1) * GM_GROUP_CH)
        for blk in range(tm // GM_BLOCK):
            rsl = slice(blk * GM_BLOCK, (blk + 1) * GM_BLOCK)
            s = _bdot(w, vn[rsl, csl]) + bs_ref[g]
            o_ref[rsl, csl] = u[rsl, csl] * s


def _gmlp(p, ln_g, ln_b, w_s, b_s):
    T = p.shape[0]
    tm = 512
    return pl.pallas_call(
        functools.partial(_gmlp_kernel, tm=tm),
        grid=(T // tm,),
        in_specs=[
            pl.BlockSpec((tm, 2 * GM_WIDTH), lambda i: (i, OFF_GM // (2 * GM_WIDTH))),
            pl.BlockSpec((1, GM_WIDTH), lambda i: (0, 0)),
            pl.BlockSpec((1, GM_WIDTH), lambda i: (0, 0)),
            pl.BlockSpec((GM_GROUPS, GM_BLOCK, GM_BLOCK), lambda i: (0, 0, 0)),
            pl.BlockSpec((GM_GROUPS, GM_BLOCK, 1), lambda i: (0, 0, 0)),
        ],
        out_specs=pl.BlockSpec((tm, GM_WIDTH), lambda i: (i, 0)),
        out_shape=jax.ShapeDtypeStruct((T, GM_WIDTH), F32),
        compiler_params=_cparams(("arbitrary",)),
        name="gmlp",
    )(p, ln_g, ln_b, w_s, b_s.reshape(GM_GROUPS, GM_BLOCK, 1))


def _split3_dot_rhs(m_bf16, x):
    x1 = x.astype(BF16)
    r1 = x - x1.astype(F32)
    x2 = r1.astype(BF16)
    x3 = (r1 - x2.astype(F32)).astype(BF16)
    return _bdot(m_bf16, x1) + _bdot(m_bf16, x2) + _bdot(m_bf16, x3)


def _head_sums(x, even):
    s_even = jnp.sum(jnp.where(even, x, 0.0), axis=-1, keepdims=True)
    s_odd = jnp.sum(jnp.where(even, 0.0, x), axis=-1, keepdims=True)
    return jnp.where(even, s_even, s_odd)


def _bd2(x0, x1):
    z = jnp.zeros_like(x0)
    return jnp.concatenate([jnp.concatenate([x0, z], axis=1), jnp.concatenate([z, x1], axis=1)], axis=0)


def _bd(x):
    xb = x.astype(BF16)
    return _bd2(xb[:, :LANE], xb[:, LANE:])


def _unit_lower_inverse(lows, row, colm, n):
    eye = jnp.where(row == colm, 1.0, 0.0)
    base = 16
    same = jnp.where((row // base) == (colm // base), 1.0, 0.0)
    pws = [low * same for low in lows]
    invs = [eye + pw for pw in pws]
    span = 2
    while span < base:
        pws = [_bdot(pw, _bd(pw)) for pw in pws]
        invs = [inv + _bdot(inv, _bd(pw)) for inv, pw in zip(invs, pws)]
        span *= 2
    size = base
    while size < n:
        pair = (row // (2 * size)) == (colm // (2 * size))
        sel = jnp.where(pair, jnp.where((row // size) == (colm // size), 0.0, 1.0), 0.0)
        tmp = [_bdot(inv, _bd(low * sel)) for inv, low in zip(invs, lows)]
        invs = [inv + _bdot(t, _bd(inv)) for inv, t in zip(invs, tmp)]
        size *= 2
    return invs


def _rwkv_kernel(*refs, C, nch, has_vmix):
    if has_vmix:
        (r_ref, k_ref, v_ref, lat_ref, vf_ref, mur_ref, muk_ref, muv_ref, mul_ref, w0_ref, w2_ref, a0_ref, a2_ref,
         kk_ref, ka_ref, v0_ref, v1_ref, v2_ref, rk_ref, g_ref, beta_ref,
         o_ref, cr_ref, ck_ref, cv_ref, cl_ref, s_ref) = refs
    else:
        (r_ref, k_ref, v_ref, lat_ref, mur_ref, muk_ref, muv_ref, mul_ref, w0_ref, w2_ref, a0_ref, a2_ref,
         kk_ref, ka_ref, rk_ref, g_ref, beta_ref,
         o_ref, vo_ref, cr_ref, ck_ref, cv_ref, cl_ref, s_ref) = refs
    P = 2 * C
    R = nch * C

    @pl.when(pl.program_id(1) == 0)
    def _():
        s_ref[...] = jnp.zeros_like(s_ref)
        for c in (cr_ref, ck_ref, cv_ref, cl_ref):
            c[...] = jnp.zeros_like(c)

    def shifted(x_ref, carry_ref, mu_ref):
        x = x_ref[...].astype(F32)
        rolled = pltpu.roll(x, 1, 0)
        rid = lax.broadcasted_iota(jnp.int32, x.shape, 0)
        prev = jnp.where(rid == 0, carry_ref[0:1, :], rolled)
        carry_ref[0:1, :] = x[R - 1:R, :]
        return x + (prev - x) * mu_ref[...]

    r = shifted(r_ref, cr_ref, mur_ref)
    k = shifted(k_ref, ck_ref, muk_ref)
    v = shifted(v_ref, cv_ref, muv_ref)
    lat = shifted(lat_ref, cl_ref, mul_ref)
    ww = w0_ref[...] + _bdot(jnp.tanh(lat), w2_ref[...])
    nw = -ww
    softplus = jnp.maximum(nw, 0.0) + jnp.log(1.0 + jnp.exp(-jnp.abs(nw)))
    lw = -jnp.exp(-softplus - 0.5)
    a = _sigmoid(a0_ref[...] + _bdot(lat, a2_ref[...]))
    if has_vmix:
        gate = _sigmoid(v0_ref[...] + _bdot(_bdot(v, v1_ref[...]), v2_ref[...]))
        v = v + (vf_ref[...] - v) * gate
    else:
        vo_ref[...] = v
    even_r = lax.broadcasted_iota(jnp.int32, (R, LANE), 1) < RW_HEAD
    kk = k * kk_ref[...]
    n2 = jnp.concatenate([_head_sums((kk * kk)[:, j * LANE:(j + 1) * LANE], even_r)
                          for j in range(RW_WIDTH // LANE)], axis=1)
    kk = kk / jnp.maximum(jnp.sqrt(n2), 1e-12)
    k = k * (1.0 + (a - 1.0) * ka_ref[...])
    a_s = -kk
    b_s = kk * a

    rowc = lax.broadcasted_iota(jnp.int32, (C, C), 0)
    colc = lax.broadcasted_iota(jnp.int32, (C, C), 1)
    tri = jnp.where(rowc >= colc, 1.0, 0.0).astype(BF16)
    cum = jnp.concatenate([_split3_dot_rhs(tri, lw[ci * C:(ci + 1) * C, :]) for ci in range(nch)], axis=0)
    e_in = jnp.exp(cum)
    e_inv = jnp.exp(-cum)
    a_t = a_s * jnp.exp(cum - lw)
    r_t = r * e_in
    b_t = b_s * e_inv
    k_t = k * e_inv
    rk = r * k * rk_ref[...]

    row = lax.broadcasted_iota(jnp.int32, (P, 2 * P), 0)
    colm = lax.broadcasted_iota(jnp.int32, (P, 2 * P), 1) % P
    same_head = (row // C) == (colm // C)
    strict = jnp.where(same_head, jnp.where((row % C) > (colm % C), 1.0, 0.0), 0.0)
    incl = jnp.where(same_head, jnp.where((row % C) >= (colm % C), 1.0, 0.0), 0.0)
    even = lax.broadcasted_iota(jnp.int32, (C, LANE), 1) < RW_HEAD

    def stack(x):
        return jnp.concatenate([jnp.where(even, x, 0.0), jnp.where(even, 0.0, x)], axis=0)

    npair = RW_HEADS // 2
    items = [(ci, pr) for ci in range(nch) for pr in range(npair)]
    rws = [slice(ci * C, (ci + 1) * C) for ci, _ in items]
    sls = [slice(pr * LANE, (pr + 1) * LANE) for _, pr in items]
    duals = [(i, i + 1) for i in range(0, len(items), 2)]
    a_ps = [stack(a_t[rw, sl]).astype(BF16) for rw, sl in zip(rws, sls)]
    r_ps = [stack(r_t[rw, sl]) for rw, sl in zip(rws, sls)]
    bks = [jnp.concatenate([stack(b_t[rw, sl]), stack(k_t[rw, sl])], axis=0) for rw, sl in zip(rws, sls)]
    v_ps = [stack(v[rw, sl]).astype(BF16) for rw, sl in zip(rws, sls)]
    gs = [_bdot(jnp.concatenate([ap, rp.astype(BF16)], axis=0), bk, NT)
          for ap, rp, bk in zip(a_ps, r_ps, bks)]
    lows = [jnp.concatenate([gs[i][:P, :P], gs[j][:P, :P]], axis=1) * strict for i, j in duals]
    tinvs = _unit_lower_inverse(lows, row, colm, C)
    akvs = [_bdot(jnp.concatenate([gs[i][:P, P:], gs[j][:P, P:]], axis=1) * strict, _bd2(v_ps[i], v_ps[j]))
            for i, j in duals]
    ws = [_bdot(tinv, _bd2(a_ps[i], a_ps[j])) for tinv, (i, j) in zip(tinvs, duals)]
    u0s = [_bdot(tinv, _bd(akv)) for tinv, akv in zip(tinvs, akvs)]
    rqs = [jnp.concatenate([r_ps[i], r_ps[j]], axis=1)
           + _bdot(jnp.concatenate([gs[i][P:, :P], gs[j][P:, :P]], axis=1) * incl, _bd(w))
           for w, (i, j) in zip(ws, duals)]
    cum_last = [cum[(ci + 1) * C - 1:(ci + 1) * C, :] for ci in range(nch)]
    e_end = [jnp.exp(cum_last[ci] - cum[ci * C:(ci + 1) * C, :]) for ci in range(nch)]
    p_end = [jnp.exp(cl) for cl in cum_last]
    bkes = [jnp.concatenate([stack(b_s[rw, sl] * e_end[ci][:, sl]), stack(k[rw, sl] * e_end[ci][:, sl])],
                            axis=0).astype(BF16)
            for (ci, _), rw, sl in zip(items, rws, sls)]
    half = lambda x, it: x[it // 2][:, (it % 2) * LANE:(it % 2 + 1) * LANE]
    uvs = [jnp.concatenate([half(u0s, it).astype(BF16), v_ps[it]], axis=0) for it in range(len(items))]
    y0s = [_bdot(gs[it][P:, :] * incl, uvs[it]) for it in range(len(items))]
    mts = [_bdot(half(ws, it), bkes[it][:P], TN).astype(BF16) for it in range(len(items))]
    nts = [_bdot(uvs[it], bkes[it], TN) for it in range(len(items))]
    state = [s_ref[pr] for pr in range(npair)]
    ys = []
    for ci in range(nch):
        base = ci * npair
        new_state = []
        for d in range(npair // 2):
            i, j = base + 2 * d, base + 2 * d + 1
            si, sj = state[2 * d], state[2 * d + 1]
            sm = _bdot(jnp.concatenate([si, sj], axis=1), _bd2(mts[i], mts[j]))
            yd = _bdot(rqs[i // 2], _bd2(si.astype(BF16), sj.astype(BF16)), NT)
            for hf, (it, s0) in enumerate(((i, si), (j, sj))):
                hs = slice(hf * LANE, (hf + 1) * LANE)
                new_state.append(s0 * p_end[ci][:, sls[it]] + sm[:, hs] + nts[it])
                y_ps = yd[:, hs] + y0s[it]
                ys.append(y_ps[:C] + y_ps[C:])
        state = new_state
    for pr in range(npair):
        s_ref[pr] = state[pr]
    mus = [_head_sums(y, even) * (1.0 / RW_HEAD) for y in ys]
    ycs = [y - mu for y, mu in zip(ys, mus)]
    vrs = [_head_sums(yc * yc, even) * (1.0 / RW_HEAD) for yc in ycs]
    bns = [_head_sums(rk[rw, sl], even) * v[rw, sl] for rw, sl in zip(rws, sls)]
    for rw, sl, yc, var, bonus in zip(rws, sls, ycs, vrs, bns):
        o_ref[rw, sl] = yc * lax.rsqrt(var + RW_LN_EPS) * g_ref[:, sl] + beta_ref[:, sl] + bonus


def _rwkv(p, v_first, prm, batch, seq):
    T = p.shape[0]
    C = CHUNK
    nch = 4
    R = nch * C
    nt = seq // R
    has_vmix = v_first is not None
    W = RW_WIDTH
    rowp = lambda blk: (lambda b, j: (b * nt + j, blk))
    row0 = lambda b, j: (b * nt + j, 0)
    const = lambda b, j: (0, 0)
    vec = pl.BlockSpec((1, W), const)
    in_specs = [
        pl.BlockSpec((R, W), rowp(OFF_R // W)),
        pl.BlockSpec((R, W), rowp(OFF_K // W)),
        pl.BlockSpec((R, W), rowp(OFF_V // W)),
        pl.BlockSpec((R, LANE), rowp(OFF_LAT // LANE)),
    ]
    args = [p, p, p, p]
    if has_vmix:
        in_specs.append(pl.BlockSpec((R, W), row0))
        args.append(v_first)
    in_specs += [vec, vec, vec, pl.BlockSpec((1, LANE), const), vec, pl.BlockSpec((LANE, W), const),
                 vec, pl.BlockSpec((LANE, W), const), vec, vec]
    args += [prm["mu_r"], prm["mu_k"], prm["mu_v"], prm["mu_l"], prm["w0"], prm["w2"], prm["a0"], prm["a2"],
             prm["k_k"], prm["k_a"]]
    if has_vmix:
        in_specs += [vec, pl.BlockSpec((W, LANE), const), pl.BlockSpec((LANE, W), const)]
        args += [prm["v0"], prm["v1"], prm["v2"]]
    in_specs += [vec, vec, vec]
    args += [prm["r_k"], prm["lnx_g"], prm["lnx_b"]]
    out = pl.BlockSpec((R, W), row0)
    n_out = 1 if has_vmix else 2
    res = pl.pallas_call(
        functools.partial(_rwkv_kernel, C=C, nch=nch, has_vmix=has_vmix),
        grid=(batch, nt),
        in_specs=in_specs,
        out_specs=[out] * n_out,
        out_shape=[jax.ShapeDtypeStruct((T, W), F32)] * n_out,
        scratch_shapes=[pltpu.VMEM((8, W), F32)] * 3 + [pltpu.VMEM((8, LANE), F32),
                                                        pltpu.VMEM((RW_HEADS // 2, LANE, LANE), F32)],
        compiler_params=_cparams(("arbitrary", "arbitrary")),
        name="rwkv7",
    )(*args)
    return (res[0], v_first) if has_vmix else (res[0], res[1])


def _merge_kernel(ya_ref, yg_ref, yr_ref, za_ref, zg_ref, zr_ref, ga_ref, gg_ref, gr_ref,
                  x_ref, gate_ref, pg_ref, wbr_ref, wout_ref, o_ref):
    acc = None
    for n, (y_ref, z_ref, g_ref) in enumerate(((ya_ref, za_ref, ga_ref), (yg_ref, zg_ref, gg_ref),
                                                (yr_ref, zr_ref, gr_ref))):
        z = z_ref[...].astype(F32)
        br = (y_ref[...] * (z * _sigmoid(z))).astype(BF16)
        pr = _bdot(br, wbr_ref[n]) * _sigmoid(g_ref[...].astype(F32))
        acc = pr if acc is None else acc + pr
    y = _bdot(acc, wout_ref[...])
    yn = y * lax.rsqrt(jnp.mean(y * y, axis=-1, keepdims=True) + EPS) * pg_ref[...]
    o_ref[...] = x_ref[...] + gate_ref[0] * yn


def _merge(y_mla, y_gm, y_rw, p, x2d, gate, post_g, w_br, w_out, layer, seq):
    T, D = x2d.shape
    tm = 512
    per_b = seq // tm
    rowp = lambda blk: (lambda i: (i, blk))
    yspec = pl.BlockSpec((tm, BW), rowp(0))
    return pl.pallas_call(
        _merge_kernel,
        grid=(T // tm,),
        in_specs=[yspec, yspec, yspec]
        + [pl.BlockSpec((tm, BW), rowp(OFF_Z // BW + n)) for n in range(3)]
        + [pl.BlockSpec((tm, D), rowp(OFF_G // D + n)) for n in range(3)]
        + [
            pl.BlockSpec((tm, D), rowp(0)),
            pl.BlockSpec((1, 1, D), lambda i: (i // per_b, 0, 0)),
            pl.BlockSpec((1, D), lambda i: (0, 0)),
            pl.BlockSpec((None, 3, BW, D), lambda i: (layer, 0, 0, 0)),
            pl.BlockSpec((None, D, D), lambda i: (layer, 0, 0)),
        ],
        out_specs=pl.BlockSpec((tm, D), rowp(0)),
        out_shape=jax.ShapeDtypeStruct((T, D), F32),
        compiler_params=_cparams(("arbitrary",)),
        name="merge",
    )(y_mla, y_gm, y_rw, p, p, p, p, p, p, x2d, gate, post_g, w_br, w_out)


def _relayout_w_in(w_in):
    L, D, _ = w_in.shape
    w = jnp.swapaxes(w_in, 1, 2).astype(BF16)
    o_q, o_kv, o_kr, o_gm = 0, MLA_Q_RANK, MLA_Q_RANK + MLA_KV_RANK, MLA_Q_RANK + MLA_KV_RANK + MLA_ROPE
    o_rw = o_gm + 2 * GM_WIDTH
    o_lat = o_rw + 3 * RW_WIDTH
    o_z = o_lat + 2 * RW_LORA
    o_g = o_z + 3 * BW
    half = MLA_ROPE // 2
    kr = w[:, o_kr:o_kr + MLA_ROPE]
    krs = jnp.concatenate([kr[:, half:], kr[:, :half]], axis=1)
    z64 = jnp.zeros((L, MLA_NOPE, D), BF16)
    z32 = jnp.zeros((L, LANE - MLA_NOPE - MLA_ROPE, D), BF16)
    pad = jnp.zeros((L, NP - OFF_LAT - LANE, D), BF16)
    rows = [
        w[:, o_gm:o_lat],
        w[:, o_z:o_g],
        w[:, o_g:],
        w[:, o_q:o_kr],
        z64, kr, z32,
        z64, krs, z32,
        w[:, o_lat:o_z],
        pad,
    ]
    return jnp.concatenate(rows, axis=1)


def _relayout_mla(w_uq, w_ukv):
    L = w_uq.shape[0]
    dq = MLA_NOPE + MLA_ROPE
    half = MLA_ROPE // 2
    wq = w_uq.astype(BF16).reshape(L, MLA_Q_RANK, MLA_HEADS, dq)
    padq = ((0, 0), (0, 0), (0, 0), (0, HEAD_PAD - dq))
    wq_p = jnp.pad(wq, padq).reshape(L, MLA_Q_RANK, MLA_HEADS * HEAD_PAD)
    rope = wq[..., MLA_NOPE:]
    rot = jnp.concatenate([jnp.zeros_like(wq[..., :MLA_NOPE]), rope[..., half:], rope[..., :half]], axis=-1)
    wqr_p = jnp.pad(rot, padq).reshape(L, MLA_Q_RANK, MLA_HEADS * HEAD_PAD)
    wkv = w_ukv.astype(BF16).reshape(L, MLA_KV_RANK, MLA_HEADS, MLA_NOPE + MLA_VDIM)
    wk_p = jnp.pad(wkv[..., :MLA_NOPE], ((0, 0), (0, 0), (0, 0), (0, HEAD_PAD - MLA_NOPE)))
    wk_p = wk_p.reshape(L, MLA_KV_RANK, MLA_HEADS * HEAD_PAD)
    wv = wkv[..., MLA_NOPE:].reshape(L, MLA_KV_RANK, MLA_HEADS // 2, 2, MLA_VDIM)
    zv = jnp.zeros_like(wv[:, :, :, 0])
    wv_p = jnp.concatenate([wv[:, :, :, 0], zv, zv, wv[:, :, :, 1]], axis=-1)
    wv_p = wv_p.reshape(L, MLA_KV_RANK, MLA_HEADS * HEAD_PAD)
    return wq_p, wqr_p, wk_p, wv_p


def kernel(x, c, positions, pre_g, post_g, w_ada, b_ada, w_in, mla_q_norm, mla_w_uq, mla_kv_norm, mla_w_ukv,
           gm_ln_g, gm_ln_b, gm_w_s, gm_b_s, rw_mu, rw_w0, rw_w2, rw_a0, rw_a2, rw_k_k, rw_k_a, rw_r_k,
           rw_lnx_g, rw_lnx_b, rw_v0, rw_v1, rw_v2, w_br, w_out):
    B, S, D = x.shape
    L = w_in.shape[0]
    T = B * S
    x2d = x.reshape(T, D)

    c_pad = jnp.pad(c, ((0, 8 - B), (0, 0)))
    mods = _ada(c_pad, w_ada, b_ada)[:, :B]
    ctab, stab = _rope_tables(positions)

    w_in_p = _relayout_w_in(w_in)
    wq_p, wqr_p, wk_p, wv_p = _relayout_mla(mla_w_uq, mla_w_ukv)
    w_br_b = w_br.astype(BF16)
    w_out_b = w_out.astype(BF16)
    zl = jnp.zeros((RW_LORA, RW_WIDTH), F32)
    pair_lane = np.arange(MLA_HEADS * HEAD_PAD) % (2 * HEAD_PAD)
    v_ones = jnp.asarray(((pair_lane >= MLA_VDIM) & (pair_lane < 2 * HEAD_PAD - MLA_VDIM)).astype(np.float32)[None, :])

    v_first = None
    for l in range(L):
        shift = mods[l, :, :D].reshape(B, 1, D)
        scale = mods[l, :, D:2 * D].reshape(B, 1, D)
        gate = mods[l, :, 2 * D:].reshape(B, 1, D)
        p = _inproj(x2d, scale, shift, pre_g[l].reshape(1, D), w_in_p, l, S)

        q, k, v = _mla_prep(p, ctab, stab, mla_q_norm[l].reshape(1, -1), mla_kv_norm[l].reshape(1, -1),
                            wq_p, wqr_p, wk_p, wv_p, v_ones, l)
        y_mla = _attention(q, k, v, B, S)

        y_gm = _gmlp(p, gm_ln_g[l].reshape(1, -1), gm_ln_b[l].reshape(1, -1), gm_w_s[l], gm_b_s[l])

        mu = rw_mu[l]
        prm = {
            "mu_r": mu[:RW_WIDTH].reshape(1, -1),
            "mu_k": mu[RW_WIDTH:2 * RW_WIDTH].reshape(1, -1),
            "mu_v": mu[2 * RW_WIDTH:3 * RW_WIDTH].reshape(1, -1),
            "mu_l": mu[3 * RW_WIDTH:].reshape(1, -1),
            "w0": rw_w0[l].reshape(1, -1),
            "w2": jnp.concatenate([rw_w2[l], zl], axis=0),
            "a0": rw_a0[l].reshape(1, -1),
            "a2": jnp.concatenate([zl, rw_a2[l]], axis=0),
            "k_k": rw_k_k[l].reshape(1, -1),
            "k_a": rw_k_a[l].reshape(1, -1),
            "r_k": rw_r_k[l].reshape(1, -1),
            "lnx_g": rw_lnx_g[l].reshape(1, -1),
            "lnx_b": rw_lnx_b[l].reshape(1, -1),
        }
        if l > 0:
            prm["v0"] = rw_v0[l - 1].reshape(1, -1)
            prm["v1"] = jnp.pad(rw_v1[l - 1], ((0, 0), (0, LANE - RW_V_LORA)))
            prm["v2"] = jnp.pad(rw_v2[l - 1], ((0, LANE - RW_V_LORA), (0, 0)))
        y_rw, v_first = _rwkv(p, v_first, prm, B, S)

        x2d = _merge(y_mla, y_gm, y_rw, p, x2d, gate, post_g[l].reshape(1, D), w_br_b, w_out_b, l, S)
    return x2d.reshape(B, S, D)
```

```python
import functools
import math

import jax
import jax.numpy as jnp
import numpy as np
from jax import lax
from jax.experimental import pallas as pl
from jax.experimental.pallas import tpu as pltpu

F32 = jnp.float32
BF16 = jnp.bfloat16

CHUNK = 64
EPS = 1e-6
LN_EPS = 1e-5
MLA_HEADS = 8
MLA_Q_RANK = 256
MLA_KV_RANK = 128
MLA_NOPE = 64
MLA_ROPE = 32
MLA_VDIM = 64
ROPE_THETA = 10000.0
GM_GROUPS = 4
GM_GROUP_CH = 128
GM_WIDTH = 512
GM_BLOCK = 128
RW_HEADS = 8
RW_HEAD = 64
RW_WIDTH = 512
RW_LORA = 64
RW_V_LORA = 32
RW_LN_EPS = 64e-5
BW = 512
LANE = 128
HEAD_PAD = 128
ATT_NH = 4

OFF_GM = 0
OFF_R, OFF_K, OFF_V = 1024, 1536, 2048
OFF_Z = 2560
OFF_G = 4096
OFF_QLAT = 7168
OFF_KVLAT = 7424
OFF_KR = 7552
OFF_KRS = 7680
OFF_LAT = 7808
NP = 8192

VMEM_LIMIT = 56 * 1024 * 1024
NT = (((1,), (1,)), ((), ()))
TN = (((0,), (0,)), ((), ()))


def _cparams(sem):
    return pltpu.CompilerParams(dimension_semantics=sem, vmem_limit_bytes=VMEM_LIMIT)


def _bdot(a, b, dims=(((1,), (0,)), ((), ()))):
    return lax.dot_general(a.astype(BF16), b.astype(BF16), dims, preferred_element_type=F32)


def _sigmoid(x):
    return 1.0 / (1.0 + jnp.exp2(x * (-math.log2(math.e))))


def _ada_kernel(c_ref, w_ref, b_ref, o_ref):
    c = c_ref[...]
    ca = c * _sigmoid(c)
    o_ref[0] = _bdot(ca, w_ref[0]) + b_ref[0]


def _ada(c_pad, w_ada, b_ada):
    L, D, D3 = w_ada.shape
    tn = 1024
    return pl.pallas_call(
        _ada_kernel,
        grid=(L, D3 // tn),
        in_specs=[
            pl.BlockSpec((8, D), lambda l, j: (0, 0)),
            pl.BlockSpec((1, D, tn), lambda l, j: (l, 0, j)),
            pl.BlockSpec((1, 1, tn), lambda l, j: (l, 0, j)),
        ],
        out_specs=pl.BlockSpec((1, 8, tn), lambda l, j: (l, 0, j)),
        out_shape=jax.ShapeDtypeStruct((L, 8, D3), F32),
        compiler_params=_cparams(("arbitrary", "arbitrary")),
        name="ada",
    )(c_pad, w_ada, b_ada.reshape(L, 1, D3))


def _rope_kernel(pos_ref, freq_ref, sign_ref, c_ref, s_ref):
    pos = pos_ref[...].astype(F32)
    ang = pos * freq_ref[...]
    lane = lax.broadcasted_iota(jnp.int32, ang.shape, 1)
    is_rope = (lane >= MLA_NOPE) & (lane < MLA_NOPE + MLA_ROPE)
    c_ref[...] = jnp.where(is_rope, jnp.cos(ang), jnp.where(lane < MLA_NOPE, 1.0, 0.0))
    s_ref[...] = jnp.sin(ang) * sign_ref[...]


def _rope_tables(positions):
    T = positions.size
    tm = 2048
    inv_freq = ROPE_THETA ** (-np.arange(0, MLA_ROPE, 2, dtype=np.float32) / MLA_ROPE)
    half = MLA_ROPE // 2
    freq = np.zeros((1, LANE), np.float32)
    freq[0, MLA_NOPE:MLA_NOPE + half] = inv_freq
    freq[0, MLA_NOPE + half:MLA_NOPE + MLA_ROPE] = inv_freq
    sign = np.zeros((1, LANE), np.float32)
    sign[0, MLA_NOPE:MLA_NOPE + half] = -1.0
    sign[0, MLA_NOPE + half:MLA_NOPE + MLA_ROPE] = 1.0
    return pl.pallas_call(
        _rope_kernel,
        grid=(T // tm,),
        in_specs=[
            pl.BlockSpec((tm, 1), lambda i: (i, 0)),
            pl.BlockSpec((1, LANE), lambda i: (0, 0)),
            pl.BlockSpec((1, LANE), lambda i: (0, 0)),
        ],
        out_specs=[pl.BlockSpec((tm, LANE), lambda i: (i, 0))] * 2,
        out_shape=[jax.ShapeDtypeStruct((T, LANE), F32)] * 2,
        compiler_params=_cparams(("arbitrary",)),
        name="rope_tables",
    )(positions.reshape(T, 1), jnp.asarray(freq), jnp.asarray(sign))


def _inproj_kernel(x_ref, sc_ref, sh_ref, g_ref, w_ref, o_ref, h_ref):
    @pl.when(pl.program_id(1) == 0)
    def _():
        x = x_ref[...]
        ms = jnp.mean(x * x, axis=-1, keepdims=True)
        y = x * lax.rsqrt(ms + EPS) * g_ref[...]
        h_ref[...] = (y * (1.0 + sc_ref[0]) + sh_ref[0]).astype(BF16)

    o_ref[...] = _bdot(h_ref[...], w_ref[...], NT).astype(BF16)


def _inproj(x2d, scale, shift, pre_g, w_in_p, layer, seq):
    T, D = x2d.shape
    tm, tn = 1024, 4096
    per_b = seq // tm
    return pl.pallas_call(
        _inproj_kernel,
        grid=(T // tm, NP // tn),
        in_specs=[
            pl.BlockSpec((tm, D), lambda i, j: (i, 0)),
            pl.BlockSpec((1, 1, D), lambda i, j: (i // per_b, 0, 0)),
            pl.BlockSpec((1, 1, D), lambda i, j: (i // per_b, 0, 0)),
            pl.BlockSpec((1, D), lambda i, j: (0, 0)),
            pl.BlockSpec((None, tn, D), lambda i, j: (layer, j, 0)),
        ],
        out_specs=pl.BlockSpec((tm, tn), lambda i, j: (i, j)),
        out_shape=jax.ShapeDtypeStruct((T, NP), BF16),
        scratch_shapes=[pltpu.VMEM((tm, D), BF16)],
        compiler_params=_cparams(("arbitrary", "arbitrary")),
        name="inproj",
    )(x2d, scale, shift, pre_g, w_in_p)


def _mla_prep_kernel(ql_ref, kvl_ref, kr_ref, krs_ref, ct_ref, st_ref, qg_ref, kvg_ref,
                     wq_ref, wqr_ref, wk_ref, wv_ref, vone_ref, q_ref, k_ref, v_ref, *, scale):
    ql = ql_ref[...].astype(F32)
    qn = (ql * lax.rsqrt(jnp.mean(ql * ql, axis=-1, keepdims=True) + EPS) * qg_ref[...]).astype(BF16)
    kvl = kvl_ref[...].astype(F32)
    kvn = (kvl * lax.rsqrt(jnp.mean(kvl * kvl, axis=-1, keepdims=True) + EPS) * kvg_ref[...]).astype(BF16)
    ct = ct_ref[...]
    st = st_ref[...]
    kpe = kr_ref[...].astype(F32) * ct + krs_ref[...].astype(F32) * st
    gw = ATT_NH * HEAD_PAD
    vfull = (_bdot(kvn, wv_ref[...]) + vone_ref[...]).astype(BF16)
    for g in range(MLA_HEADS // ATT_NH):
        v_ref[g] = vfull[:, g * gw:(g + 1) * gw]
    for h in range(MLA_HEADS):
        sl = slice(h * HEAD_PAD, (h + 1) * HEAD_PAD)
        gsl = slice((h % ATT_NH) * HEAD_PAD, (h % ATT_NH + 1) * HEAD_PAD)
        q = _bdot(qn, wq_ref[:, sl]) * ct + _bdot(qn, wqr_ref[:, sl]) * st
        q_ref[h // ATT_NH, :, gsl] = (q * scale).astype(BF16)
        k_ref[h // ATT_NH, :, gsl] = (_bdot(kvn, wk_ref[:, sl]) + kpe).astype(BF16)


def _mla_prep(p, ctab, stab, qg, kvg, wq, wqr, wk, wv, vone, layer):
    T = p.shape[0]
    tm = 512
    HP = MLA_HEADS * HEAD_PAD
    scale = float((MLA_NOPE + MLA_ROPE) ** -0.5) * math.log2(math.e)
    row = lambda blk: (lambda i: (i, blk))
    const = lambda i: (0, 0)
    wspec = lambda rows: pl.BlockSpec((None, rows, HP), lambda i: (layer, 0, 0))
    return pl.pallas_call(
        functools.partial(_mla_prep_kernel, scale=scale),
        grid=(T // tm,),
        in_specs=[
            pl.BlockSpec((tm, MLA_Q_RANK), row(OFF_QLAT // MLA_Q_RANK)),
            pl.BlockSpec((tm, LANE), row(OFF_KVLAT // LANE)),
            pl.BlockSpec((tm, LANE), row(OFF_KR // LANE)),
            pl.BlockSpec((tm, LANE), row(OFF_KRS // LANE)),
            pl.BlockSpec((tm, LANE), row(0)),
            pl.BlockSpec((tm, LANE), row(0)),
            pl.BlockSpec((1, MLA_Q_RANK), const),
            pl.BlockSpec((1, MLA_KV_RANK), const),
            wspec(MLA_Q_RANK),
            wspec(MLA_Q_RANK),
            wspec(MLA_KV_RANK),
            wspec(MLA_KV_RANK),
            pl.BlockSpec((1, HP), const),
        ],
        out_specs=[pl.BlockSpec((MLA_HEADS // ATT_NH, tm, ATT_NH * HEAD_PAD), lambda i: (0, i, 0))] * 3,
        out_shape=[jax.ShapeDtypeStruct((MLA_HEADS // ATT_NH, T, ATT_NH * HEAD_PAD), BF16)] * 3,
        compiler_params=_cparams(("arbitrary",)),
        name="mla_prep",
    )(p, p, p, p, ctab, stab, qg, kvg, wq, wqr, wk, wv, vone)


def _attn_kernel(q_ref, k_ref, v_ref, o_ref, s_ref, p_ref, m_ref, al_ref, acc_ref, *, tq, nh):
    qi = pl.program_id(2)
    m_ref[...] = jnp.full(m_ref.shape, -1e30, F32)
    acc_ref[...] = jnp.zeros(acc_ref.shape, F32)
    visible = (lax.broadcasted_iota(jnp.int32, (tq, tq), 1) // CHUNK
               <= lax.broadcasted_iota(jnp.int32, (tq, tq), 0) // CHUNK)

    def kv_step(ki, masked):
        start = pl.multiple_of(ki * tq, tq)
        for hh in range(nh):
            hsl = slice(hh * HEAD_PAD, (hh + 1) * HEAD_PAD)
            s = _bdot(q_ref[:, hsl], k_ref[pl.ds(start, tq), hsl], NT)
            if masked:
                s = jnp.where(visible, s, -1e30)
            s_ref[hh] = s
            m_old = m_ref[hh]
            m_new = jnp.maximum(m_old, jnp.max(s, axis=-1, keepdims=True))
            al_ref[hh] = jnp.exp2(m_old - m_new)
            m_ref[hh] = m_new
        for hh in range(nh):
            m_new = m_ref[hh]
            for j in range(tq // LANE):
                csl = slice(j * LANE, (j + 1) * LANE)
                p_ref[hh, :, csl] = jnp.exp2(s_ref[hh, :, csl] - m_new).astype(BF16)
        for hh in range(nh):
            vsl = slice(hh * LANE, (hh + 1) * LANE)
            acc_ref[hh] = al_ref[hh] * acc_ref[hh] + _bdot(p_ref[hh], v_ref[pl.ds(start, tq), vsl])

    def full_step(ki, carry):
        kv_step(ki, False)
        return carry

    lax.fori_loop(0, qi, full_step, 0)
    kv_step(qi, True)
    lane = lax.broadcasted_iota(jnp.int32, (tq, LANE), 1)
    for pr in range(nh // 2):
        a0 = acc_ref[2 * pr]
        a1 = acc_ref[2 * pr + 1]
        l0 = a0[:, MLA_VDIM:MLA_VDIM + 1]
        l1 = a1[:, 0:1]
        o_ref[:, pr * LANE:(pr + 1) * LANE] = jnp.where(lane < MLA_VDIM, a0 / l0, a1 / l1)


def _attention(q, k, v, batch, seq):
    T = q.shape[1]
    tq = 512
    nh = ATT_NH
    nq = seq // tq
    return pl.pallas_call(
        functools.partial(_attn_kernel, tq=tq, nh=nh),
        grid=(batch, MLA_HEADS // nh, nq),
        in_specs=[
            pl.BlockSpec((None, tq, nh * HEAD_PAD), lambda b, h, i: (h, b * nq + i, 0)),
            pl.BlockSpec((None, seq, nh * HEAD_PAD), lambda b, h, i: (h, b, 0)),
            pl.BlockSpec((None, seq, nh * HEAD_PAD), lambda b, h, i: (h, b, 0)),
        ],
        out_specs=pl.BlockSpec((tq, nh * MLA_VDIM), lambda b, h, i: (b * nq + i, h)),
        out_shape=jax.ShapeDtypeStruct((T, MLA_HEADS * MLA_VDIM), F32),
        scratch_shapes=[
            pltpu.VMEM((nh, tq, tq), F32),
            pltpu.VMEM((nh, tq, tq), BF16),
            pltpu.VMEM((nh, tq, LANE), F32),
            pltpu.VMEM((nh, tq, LANE), F32),
            pltpu.VMEM((nh, tq, LANE), F32),
        ],
        compiler_params=_cparams(("arbitrary", "arbitrary", "arbitrary")),
        name="attention",
    )(q, k, v)


def _gmlp_kernel(p_ref, g_ref, b_ref, w_ref, bs_ref, o_ref, *, tm):
    x = p_ref[...].astype(F32)
    ge = 0.5 * x * (1.0 + lax.erf(x * (1.0 / math.sqrt(2.0))))
    u = ge[:, :GM_WIDTH]
    v = ge[:, GM_WIDTH:]
    mu = jnp.mean(v, axis=-1, keepdims=True)
    vc = v - mu
    var = jnp.mean(vc * vc, axis=-1, keepdims=True)
    vn = (vc * lax.rsqrt(var + LN_EPS) * g_ref[...] + b_ref[...]).astype(BF16)
    row = lax.broadcasted_iota(jnp.int32, (GM_BLOCK, GM_BLOCK), 0)
    col = lax.broadcasted_iota(jnp.int32, (GM_BLOCK, GM_BLOCK), 1)
    mask = (col // CHUNK) <= (row // CHUNK)
    for g in range(GM_GROUPS):
        w = jnp.where(mask, w_ref[g], 0.0).astype(BF16)
        csl = slice(g * GM_GROUP_CH, (g + 1) * GM_GROUP_CH)
        for blk in range(tm // GM_BLOCK):
            rsl = slice(blk * GM_BLOCK, (blk + 1) * GM_BLOCK)
            s = _bdot(w, vn[rsl, csl]) + bs_ref[g]
            o_ref[rsl, csl] = u[rsl, csl] * s


def _gmlp(p, ln_g, ln_b, w_s, b_s):
    T = p.shape[0]
    tm = 512
    return pl.pallas_call(
        functools.partial(_gmlp_kernel, tm=tm),
        grid=(T // tm,),
        in_specs=[
            pl.BlockSpec((tm, 2 * GM_WIDTH), lambda i: (i, OFF_GM // (2 * GM_WIDTH))),
            pl.BlockSpec((1, GM_WIDTH), lambda i: (0, 0)),
            pl.BlockSpec((1, GM_WIDTH), lambda i: (0, 0)),
            pl.BlockSpec((GM_GROUPS, GM_BLOCK, GM_BLOCK), lambda i: (0, 0, 0)),
            pl.BlockSpec((GM_GROUPS, GM_BLOCK, 1), lambda i: (0, 0, 0)),
        ],
        out_specs=pl.BlockSpec((tm, GM_WIDTH), lambda i: (i, 0)),
        out_shape=jax.ShapeDtypeStruct((T, GM_WIDTH), F32),
        compiler_params=_cparams(("arbitrary",)),
        name="gmlp",
    )(p, ln_g, ln_b, w_s, b_s.reshape(GM_GROUPS, GM_BLOCK, 1))


def _split3_dot_rhs(m_bf16, x):
    x1 = x.astype(BF16)
    r1 = x - x1.astype(F32)
    x2 = r1.astype(BF16)
    x3 = (r1 - x2.astype(F32)).astype(BF16)
    return _bdot(m_bf16, x1) + _bdot(m_bf16, x2) + _bdot(m_bf16, x3)


def _head_sums(x, even):
    s_even = jnp.sum(jnp.where(even, x, 0.0), axis=-1, keepdims=True)
    s_odd = jnp.sum(jnp.where(even, 0.0, x), axis=-1, keepdims=True)
    return jnp.where(even, s_even, s_odd)


def _bd2(x0, x1):
    z = jnp.zeros_like(x0)
    return jnp.concatenate([jnp.concatenate([x0, z], axis=1), jnp.concatenate([z, x1], axis=1)], axis=0)


def _bd(x):
    xb = x.astype(BF16)
    return _bd2(xb[:, :LANE], xb[:, LANE:])


def _unit_lower_inverse(lows, row, colm, n):
    eye = jnp.where(row == colm, 1.0, 0.0)
    base = 16
    same = jnp.where((row // base) == (colm // base), 1.0, 0.0)
    pws = [low * same for low in lows]
    invs = [eye + pw for pw in pws]
    span = 2
    while span < base:
        pws = [_bdot(pw, _bd(pw)) for pw in pws]
        invs = [inv + _bdot(inv, _bd(pw)) for inv, pw in zip(invs, pws)]
        span *= 2
    size = base
    while size < n:
        pair = (row // (2 * size)) == (colm // (2 * size))
        sel = jnp.where(pair, jnp.where((row // size) == (colm // size), 0.0, 1.0), 0.0)
        tmp = [_bdot(inv, _bd(low * sel)) for inv, low in zip(invs, lows)]
        invs = [inv + _bdot(t, _bd(inv)) for inv, t in zip(invs, tmp)]
        size *= 2
    return invs


def _rwkv_kernel(*refs, C, nch, has_vmix):
    if has_vmix:
        (r_ref, k_ref, v_ref, lat_ref, vf_ref, mur_ref, muk_ref, muv_ref, mul_ref, w0_ref, w2_ref, a0_ref, a2_ref,
         kk_ref, ka_ref, v0_ref, v1_ref, v2_ref, rk_ref, g_ref, beta_ref,
         o_ref, cr_ref, ck_ref, cv_ref, cl_ref, s_ref) = refs
    else:
        (r_ref, k_ref, v_ref, lat_ref, mur_ref, muk_ref, muv_ref, mul_ref, w0_ref, w2_ref, a0_ref, a2_ref,
         kk_ref, ka_ref, rk_ref, g_ref, beta_ref,
         o_ref, vo_ref, cr_ref, ck_ref, cv_ref, cl_ref, s_ref) = refs
    P = 2 * C
    R = nch * C

    @pl.when(pl.program_id(1) == 0)
    def _():
        s_ref[...] = jnp.zeros_like(s_ref)
        for c in (cr_ref, ck_ref, cv_ref, cl_ref):
            c[...] = jnp.zeros_like(c)

    def shifted(x_ref, carry_ref, mu_ref):
        x = x_ref[...].astype(F32)
        rolled = pltpu.roll(x, 1, 0)
        rid = lax.broadcasted_iota(jnp.int32, x.shape, 0)
        prev = jnp.where(rid == 0, carry_ref[0:1, :], rolled)
        carry_ref[0:1, :] = x[R - 1:R, :]
        return x + (prev - x) * mu_ref[...]

    r = shifted(r_ref, cr_ref, mur_ref)
    k = shifted(k_ref, ck_ref, muk_ref)
    v = shifted(v_ref, cv_ref, muv_ref)
    lat = shifted(lat_ref, cl_ref, mul_ref)
    ww = w0_ref[...] + _bdot(jnp.tanh(lat), w2_ref[...])
    nw = -ww
    softplus = jnp.maximum(nw, 0.0) + jnp.log(1.0 + jnp.exp(-jnp.abs(nw)))
    lw = -jnp.exp(-softplus - 0.5)
    a = _sigmoid(a0_ref[...] + _bdot(lat, a2_ref[...]))
    if has_vmix:
        gate = _sigmoid(v0_ref[...] + _bdot(_bdot(v, v1_ref[...]), v2_ref[...]))
        v = v + (vf_ref[...] - v) * gate
    else:
        vo_ref[...] = v
    even_r = lax.broadcasted_iota(jnp.int32, (R, LANE), 1) < RW_HEAD
    kk = k * kk_ref[...]
    n2 = jnp.concatenate([_head_sums((kk * kk)[:, j * LANE:(j + 1) * LANE], even_r)
                          for j in range(RW_WIDTH // LANE)], axis=1)
    kk = kk / jnp.maximum(jnp.sqrt(n2), 1e-12)
    k = k * (1.0 + (a - 1.0) * ka_ref[...])
    a_s = -kk
    b_s = kk * a

    rowc = lax.broadcasted_iota(jnp.int32, (C, C), 0)
    colc = lax.broadcasted_iota(jnp.int32, (C, C), 1)
    tri = jnp.where(rowc >= colc, 1.0, 0.0).astype(BF16)
    cum = jnp.concatenate([_split3_dot_rhs(tri, lw[ci * C:(ci + 1) * C, :]) for ci in range(nch)], axis=0)
    e_in = jnp.exp(cum)
    e_inv = jnp.exp(-cum)
    a_t = a_s * jnp.exp(cum - lw)
    r_t = r * e_in
    b_t = b_s * e_inv
    k_t = k * e_inv
    rk = r * k * rk_ref[...]

    row = lax.broadcasted_iota(jnp.int32, (P, 2 * P), 0)
    colm = lax.broadcasted_iota(jnp.int32, (P, 2 * P), 1) % P
    same_head = (row // C) == (colm // C)
    strict = jnp.where(same_head, jnp.where((row % C) > (colm % C), 1.0, 0.0), 0.0)
    incl = jnp.where(same_head, jnp.where((row % C) >= (colm % C), 1.0, 0.0), 0.0)
    even = lax.broadcasted_iota(jnp.int32, (C, LANE), 1) < RW_HEAD

    def stack(x):
        return jnp.concatenate([jnp.where(even, x, 0.0), jnp.where(even, 0.0, x)], axis=0)

    npair = RW_HEADS // 2
    items = [(ci, pr) for ci in range(nch) for pr in range(npair)]
    rws = [slice(ci * C, (ci + 1) * C) for ci, _ in items]
    sls = [slice(pr * LANE, (pr + 1) * LANE) for _, pr in items]
    duals = [(i, i + 1) for i in range(0, len(items), 2)]
    a_ps = [stack(a_t[rw, sl]).astype(BF16) for rw, sl in zip(rws, sls)]
    r_ps = [stack(r_t[rw, sl]) for rw, sl in zip(rws, sls)]
    bks = [jnp.concatenate([stack(b_t[rw, sl]), stack(k_t[rw, sl])], axis=0) for rw, sl in zip(rws, sls)]
    v_ps = [stack(v[rw, sl]).astype(BF16) for rw, sl in zip(rws, sls)]
    gs = [_bdot(jnp.concatenate([ap, rp.astype(BF16)], axis=0), bk, NT)
          for ap, rp, bk in zip(a_ps, r_ps, bks)]
    lows = [jnp.concatenate([gs[i][:P, :P], gs[j][:P, :P]], axis=1) * strict for i, j in duals]
    tinvs = _unit_lower_inverse(lows, row, colm, C)
    akvs = [_bdot(jnp.concatenate([gs[i][:P, P:], gs[j][:P, P:]], axis=1) * strict, _bd2(v_ps[i], v_ps[j]))
            for i, j in duals]
    ws = [_bdot(tinv, _bd2(a_ps[i], a_ps[j])) for tinv, (i, j) in zip(tinvs, duals)]
    u0s = [_bdot(tinv, _bd(akv)) for tinv, akv in zip(tinvs, akvs)]
    rqs = [jnp.concatenate([r_ps[i], r_ps[j]], axis=1)
           + _bdot(jnp.concatenate([gs[i][P:, :P], gs[j][P:, :P]], axis=1) * incl, _bd(w))
           for w, (i, j) in zip(ws, duals)]
    cum_last = [cum[(ci + 1) * C - 1:(ci + 1) * C, :] for ci in range(nch)]
    e_end = [jnp.exp(cum_last[ci] - cum[ci * C:(ci + 1) * C, :]) for ci in range(nch)]
    p_end = [jnp.exp(cl) for cl in cum_last]
    bkes = [jnp.concatenate([stack(b_s[rw, sl] * e_end[ci][:, sl]), stack(k[rw, sl] * e_end[ci][:, sl])],
                            axis=0).astype(BF16)
            for (ci, _), rw, sl in zip(items, rws, sls)]
    half = lambda x, it: x[it // 2][:, (it % 2) * LANE:(it % 2 + 1) * LANE]
    uvs = [jnp.concatenate([half(u0s, it).astype(BF16), v_ps[it]], axis=0) for it in range(len(items))]
    y0s = [_bdot(gs[it][P:, :] * incl, uvs[it]) for it in range(len(items))]
    mts = [_bdot(half(ws, it), bkes[it][:P], TN).astype(BF16) for it in range(len(items))]
    nts = [_bdot(uvs[it], bkes[it], TN) for it in range(len(items))]
    state = [s_ref[pr] for pr in range(npair)]
    ys = []
    for ci in range(nch):
        base = ci * npair
        new_state = []
        for d in range(npair // 2):
            i, j = base + 2 * d, base + 2 * d + 1
            si, sj = state[2 * d], state[2 * d + 1]
            sm = _bdot(jnp.concatenate([si, sj], axis=1), _bd2(mts[i], mts[j]))
            yd = _bdot(rqs[i // 2], _bd2(si.astype(BF16), sj.astype(BF16)), NT)
            for hf, (it, s0) in enumerate(((i, si), (j, sj))):
                hs = slice(hf * LANE, (hf + 1) * LANE)
                new_state.append(s0 * p_end[ci][:, sls[it]] + sm[:, hs] + nts[it])
                y_ps = yd[:, hs] + y0s[it]
                ys.append(y_ps[:C] + y_ps[C:])
        state = new_state
    for pr in range(npair):
        s_ref[pr] = state[pr]
    mus = [_head_sums(y, even) * (1.0 / RW_HEAD) for y in ys]
    ycs = [y - mu for y, mu in zip(ys, mus)]
    vrs = [_head_sums(yc * yc, even) * (1.0 / RW_HEAD) for yc in ycs]
    bns = [_head_sums(rk[rw, sl], even) * v[rw, sl] for rw, sl in zip(rws, sls)]
    for rw, sl, yc, var, bonus in zip(rws, sls, ycs, vrs, bns):
        o_ref[rw, sl] = yc * lax.rsqrt(var + RW_LN_EPS) * g_ref[:, sl] + beta_ref[:, sl] + bonus


def _rwkv(p, v_first, prm, batch, seq):
    T = p.shape[0]
    C = CHUNK
    nch = 4
    R = nch * C
    nt = seq // R
    has_vmix = v_first is not None
    W = RW_WIDTH
    rowp = lambda blk: (lambda b, j: (b * nt + j, blk))
    row0 = lambda b, j: (b * nt + j, 0)
    const = lambda b, j: (0, 0)
    vec = pl.BlockSpec((1, W), const)
    in_specs = [
        pl.BlockSpec((R, W), rowp(OFF_R // W)),
        pl.BlockSpec((R, W), rowp(OFF_K // W)),
        pl.BlockSpec((R, W), rowp(OFF_V // W)),
        pl.BlockSpec((R, LANE), rowp(OFF_LAT // LANE)),
    ]
    args = [p, p, p, p]
    if has_vmix:
        in_specs.append(pl.BlockSpec((R, W), row0))
        args.append(v_first)
    in_specs += [vec, vec, vec, pl.BlockSpec((1, LANE), const), vec, pl.BlockSpec((LANE, W), const),
                 vec, pl.BlockSpec((LANE, W), const), vec, vec]
    args += [prm["mu_r"], prm["mu_k"], prm["mu_v"], prm["mu_l"], prm["w0"], prm["w2"], prm["a0"], prm["a2"],
             prm["k_k"], prm["k_a"]]
    if has_vmix:
        in_specs += [vec, pl.BlockSpec((W, LANE), const), pl.BlockSpec((LANE, W), const)]
        args += [prm["v0"], prm["v1"], prm["v2"]]
    in_specs += [vec, vec, vec]
    args += [prm["r_k"], prm["lnx_g"], prm["lnx_b"]]
    out = pl.BlockSpec((R, W), row0)
    n_out = 1 if has_vmix else 2
    res = pl.pallas_call(
        functools.partial(_rwkv_kernel, C=C, nch=nch, has_vmix=has_vmix),
        grid=(batch, nt),
        in_specs=in_specs,
        out_specs=[out] * n_out,
        out_shape=[jax.ShapeDtypeStruct((T, W), F32)] * n_out,
        scratch_shapes=[pltpu.VMEM((8, W), F32)] * 3 + [pltpu.VMEM((8, LANE), F32),
                                                        pltpu.VMEM((RW_HEADS // 2, LANE, LANE), F32)],
        compiler_params=_cparams(("arbitrary", "arbitrary")),
        name="rwkv7",
    )(*args)
    return (res[0], v_first) if has_vmix else (res[0], res[1])


def _merge_kernel(ya_ref, yg_ref, yr_ref, za_ref, zg_ref, zr_ref, ga_ref, gg_ref, gr_ref,
                  x_ref, gate_ref, pg_ref, wbr_ref, wout_ref, o_ref):
    acc = None
    for n, (y_ref, z_ref, g_ref) in enumerate(((ya_ref, za_ref, ga_ref), (yg_ref, zg_ref, gg_ref),
                                                (yr_ref, zr_ref, gr_ref))):
        z = z_ref[...].astype(F32)
        br = (y_ref[...] * (z * _sigmoid(z))).astype(BF16)
        pr = _bdot(br, wbr_ref[n]) * _sigmoid(g_ref[...].astype(F32))
        acc = pr if acc is None else acc + pr
    y = _bdot(acc, wout_ref[...])
    yn = y * lax.rsqrt(jnp.mean(y * y, axis=-1, keepdims=True) + EPS) * pg_ref[...]
    o_ref[...] = x_ref[...] + gate_ref[0] * yn


def _merge(y_mla, y_gm, y_rw, p, x2d, gate, post_g, w_br, w_out, layer, seq):
    T, D = x2d.shape
    tm = 512
    per_b = seq // tm
    rowp = lambda blk: (lambda i: (i, blk))
    yspec = pl.BlockSpec((tm, BW), rowp(0))
    return pl.pallas_call(
        _merge_kernel,
        grid=(T // tm,),
        in_specs=[yspec, yspec, yspec]
        + [pl.BlockSpec((tm, BW), rowp(OFF_Z // BW + n)) for n in range(3)]
        + [pl.BlockSpec((tm, D), rowp(OFF_G // D + n)) for n in range(3)]
        + [
            pl.BlockSpec((tm, D), rowp(0)),
            pl.BlockSpec((1, 1, D), lambda i: (i // per_b, 0, 0)),
            pl.BlockSpec((1, D), lambda i: (0, 0)),
            pl.BlockSpec((None, 3, BW, D), lambda i: (layer, 0, 0, 0)),
            pl.BlockSpec((None, D, D), lambda i: (layer, 0, 0)),
        ],
        out_specs=pl.BlockSpec((tm, D), rowp(0)),
        out_shape=jax.ShapeDtypeStruct((T, D), F32),
        compiler_params=_cparams(("arbitrary",)),
        name="merge",
    )(y_mla, y_gm, y_rw, p, p, p, p, p, p, x2d, gate, post_g, w_br, w_out)


def _relayout_w_in(w_in):
    L, D, _ = w_in.shape
    w = jnp.swapaxes(w_in, 1, 2).astype(BF16)
    o_q, o_kv, o_kr, o_gm = 0, MLA_Q_RANK, MLA_Q_RANK + MLA_KV_RANK, MLA_Q_RANK + MLA_KV_RANK + MLA_ROPE
    o_rw = o_gm + 2 * GM_WIDTH
    o_lat = o_rw + 3 * RW_WIDTH
    o_z = o_lat + 2 * RW_LORA
    o_g = o_z + 3 * BW
    half = MLA_ROPE // 2
    kr = w[:, o_kr:o_kr + MLA_ROPE]
    krs = jnp.concatenate([kr[:, half:], kr[:, :half]], axis=1)
    z64 = jnp.zeros((L, MLA_NOPE, D), BF16)
    z32 = jnp.zeros((L, LANE - MLA_NOPE - MLA_ROPE, D), BF16)
    pad = jnp.zeros((L, NP - OFF_LAT - LANE, D), BF16)
    rows = [
        w[:, o_gm:o_lat],
        w[:, o_z:o_g],
        w[:, o_g:],
        w[:, o_q:o_kr],
        z64, kr, z32,
        z64, krs, z32,
        w[:, o_lat:o_z],
        pad,
    ]
    return jnp.concatenate(rows, axis=1)


def _relayout_mla(w_uq, w_ukv):
    L = w_uq.shape[0]
    dq = MLA_NOPE + MLA_ROPE
    half = MLA_ROPE // 2
    wq = w_uq.astype(BF16).reshape(L, MLA_Q_RANK, MLA_HEADS, dq)
    padq = ((0, 0), (0, 0), (0, 0), (0, HEAD_PAD - dq))
    wq_p = jnp.pad(wq, padq).reshape(L, MLA_Q_RANK, MLA_HEADS * HEAD_PAD)
    rope = wq[..., MLA_NOPE:]
    rot = jnp.concatenate([jnp.zeros_like(wq[..., :MLA_NOPE]), rope[..., half:], rope[..., :half]], axis=-1)
    wqr_p = jnp.pad(rot, padq).reshape(L, MLA_Q_RANK, MLA_HEADS * HEAD_PAD)
    wkv = w_ukv.astype(BF16).reshape(L, MLA_KV_RANK, MLA_HEADS, MLA_NOPE + MLA_VDIM)
    wk_p = jnp.pad(wkv[..., :MLA_NOPE], ((0, 0), (0, 0), (0, 0), (0, HEAD_PAD - MLA_NOPE)))
    wk_p = wk_p.reshape(L, MLA_KV_RANK, MLA_HEADS * HEAD_PAD)
    wv = wkv[..., MLA_NOPE:].reshape(L, MLA_KV_RANK, MLA_HEADS // 2, 2, MLA_VDIM)
    zv = jnp.zeros_like(wv[:, :, :, 0])
    wv_p = jnp.concatenate([wv[:, :, :, 0], zv, zv, wv[:, :, :, 1]], axis=-1)
    wv_p = wv_p.reshape(L, MLA_KV_RANK, MLA_HEADS * HEAD_PAD)
    return wq_p, wqr_p, wk_p, wv_p


def kernel(x, c, positions, pre_g, post_g, w_ada, b_ada, w_in, mla_q_norm, mla_w_uq, mla_kv_norm, mla_w_ukv,
           gm_ln_g, gm_ln_b, gm_w_s, gm_b_s, rw_mu, rw_w0, rw_w2, rw_a0, rw_a2, rw_k_k, rw_k_a, rw_r_k,
           rw_lnx_g, rw_lnx_b, rw_v0, rw_v1, rw_v2, w_br, w_out):
    B, S, D = x.shape
    L = w_in.shape[0]
    T = B * S
    x2d = x.reshape(T, D)

    c_pad = jnp.pad(c, ((0, 8 - B), (0, 0)))
    mods = _ada(c_pad, w_ada, b_ada)[:, :B]
    ctab, stab = _rope_tables(positions)

    w_in_p = _relayout_w_in(w_in)
    wq_p, wqr_p, wk_p, wv_p = _relayout_mla(mla_w_uq, mla_w_ukv)
    w_br_b = w_br.astype(BF16)
    w_out_b = w_out.astype(BF16)
    zl = jnp.zeros((RW_LORA, RW_WIDTH), F32)
    pair_lane = np.arange(MLA_HEADS * HEAD_PAD) % (2 * HEAD_PAD)
    v_ones = jnp.asarray(((pair_lane >= MLA_VDIM) & (pair_lane < 2 * HEAD_PAD - MLA_VDIM)).astype(np.float32)[None, :])

    v_first = None
    for l in range(L):
        shift = mods[l, :, :D].reshape(B, 1, D)
        scale = mods[l, :, D:2 * D].reshape(B, 1, D)
        gate = mods[l, :, 2 * D:].reshape(B, 1, D)
        p = _inproj(x2d, scale, shift, pre_g[l].reshape(1, D), w_in_p, l, S)

        q, k, v = _mla_prep(p, ctab, stab, mla_q_norm[l].reshape(1, -1), mla_kv_norm[l].reshape(1, -1),
                            wq_p, wqr_p, wk_p, wv_p, v_ones, l)
        y_mla = _attention(q, k, v, B, S)

        y_gm = _gmlp(p, gm_ln_g[l].reshape(1, -1), gm_ln_b[l].reshape(1, -1), gm_w_s[l], gm_b_s[l])

        mu = rw_mu[l]
        prm = {
            "mu_r": mu[:RW_WIDTH].reshape(1, -1),
            "mu_k": mu[RW_WIDTH:2 * RW_WIDTH].reshape(1, -1),
            "mu_v": mu[2 * RW_WIDTH:3 * RW_WIDTH].reshape(1, -1),
            "mu_l": mu[3 * RW_WIDTH:].reshape(1, -1),
            "w0": rw_w0[l].reshape(1, -1),
            "w2": jnp.concatenate([rw_w2[l], zl], axis=0),
            "a0": rw_a0[l].reshape(1, -1),
            "a2": jnp.concatenate([zl, rw_a2[l]], axis=0),
            "k_k": rw_k_k[l].reshape(1, -1),
            "k_a": rw_k_a[l].reshape(1, -1),
            "r_k": rw_r_k[l].reshape(1, -1),
            "lnx_g": rw_lnx_g[l].reshape(1, -1),
            "lnx_b": rw_lnx_b[l].reshape(1, -1),
        }
        if l > 0:
            prm["v0"] = rw_v0[l - 1].reshape(1, -1)
            prm["v1"] = jnp.pad(rw_v1[l - 1], ((0, 0), (0, LANE - RW_V_LORA)))
            prm["v2"] = jnp.pad(rw_v2[l - 1], ((0, LANE - RW_V_LORA), (0, 0)))
        y_rw, v_first = _rwkv(p, v_first, prm, B, S)

        x2d = _merge(y_mla, y_gm, y_rw, p, x2d, gate, post_g[l].reshape(1, D), w_br_b, w_out_b, l, S)
    return x2d.reshape(B, S, D)
```
